```python
import jax, jax.numpy as jnp
from jax import lax
import numpy as np

D_MODEL = 4096
BATCH = 2
SEQ = 4096
DEPTH = 4

N_A_LAYERS = DEPTH // 2
N_B_LAYERS = DEPTH - N_A_LAYERS
MLSTM_HEADS = 8
MLSTM_V_DIM = D_MODEL // MLSTM_HEADS
MLSTM_QK_DIM = MLSTM_V_DIM // 2
MLSTM_CHUNK = 64
GATE_SOFTCAP = 15.0
QK_W = MLSTM_HEADS * MLSTM_QK_DIM
V_W = MLSTM_HEADS * MLSTM_V_DIM
A_IN_W = 2 * QK_W + 2 * V_W + 2 * MLSTM_HEADS
SB_HEADS = 32
SB_HEAD_DIM = D_MODEL // SB_HEADS
SB_BLOCK = 128
FFN_DIM = ((D_MODEL * 8 // 3 + 63) // 64) * 64
CONV_WIDTH = 3
PLE_DIM = 256
NORM_EPS = 1e-6

kernel_name = "yoco_mlstm_stickbreaking_convffn_trunk"


def rmsnorm(x, g):
    xf = x.astype(jnp.float32)
    y = xf * lax.rsqrt(jnp.mean(xf * xf, axis=-1, keepdims=True) + NORM_EPS)
    return (y * g.astype(jnp.float32)).astype(x.dtype)


def softcap(x, cap):
    return cap * jnp.tanh(x / cap)


def mlstm(xn, w_in, gate_bias, head_norm, w_out):
    B, S, _ = xn.shape
    NH, DK, DV, L = MLSTM_HEADS, MLSTM_QK_DIM, MLSTM_V_DIM, MLSTM_CHUNK
    NC = S // L
    proj = (xn @ w_in).astype(jnp.float32)
    q, k, v, og, ig, fg = jnp.split(
        proj, [QK_W, 2 * QK_W, 2 * QK_W + V_W, 2 * QK_W + 2 * V_W, 2 * QK_W + 2 * V_W + NH], axis=-1)
    gb = gate_bias.astype(jnp.float32)
    log_i = softcap(ig + gb[:NH], GATE_SOFTCAP)
    log_f = jax.nn.log_sigmoid(softcap(fg + gb[NH:], GATE_SOFTCAP))

    def to_chunks(t, d):
        return t.reshape(B, NC, L, NH, d).transpose(1, 0, 3, 2, 4)

    qc = to_chunks(q * (DK ** -0.5), DK)
    kc = to_chunks(k, DK)
    vc = to_chunks(v, DV)
    ic = log_i.reshape(B, NC, L, NH).transpose(1, 0, 3, 2)
    fc = log_f.reshape(B, NC, L, NH).transpose(1, 0, 3, 2)
    causal = jnp.tril(jnp.ones((L, L), dtype=bool))

    def step(carry, inp):
        C, n, m = carry
        qb, kb, vb, ib, fb = inp
        b = jnp.cumsum(fb, axis=-1)
        g = b[..., -1]
        Dm = jnp.where(causal, b[..., :, None] - b[..., None, :] + ib[..., None, :], -jnp.inf)
        m_inter = b + m[..., None]
        m_t = jnp.maximum(m_inter, jnp.max(Dm, axis=-1))
        w_intra = jnp.exp(Dm - m_t[..., None])
        w_inter = jnp.exp(m_inter - m_t)
        P = jnp.einsum('bhtd,bhsd->bhts', qb, kb) * w_intra
        num = (w_inter[..., None] * jnp.einsum('bhtd,bhdv->bhtv', qb, C)
               + jnp.einsum('bhts,bhsv->bhtv', P, vb))
        den = w_inter * jnp.einsum('bhtd,bhd->bht', qb, n) + jnp.sum(P, axis=-1)
        h = num / jnp.maximum(jnp.abs(den), jnp.exp(-m_t))[..., None]
        a = g[..., None] - b + ib
        m_new = jnp.maximum(g + m, jnp.max(a, axis=-1))
        wa = jnp.exp(a - m_new[..., None])
        decay = jnp.exp(g + m - m_new)
        C_new = decay[..., None, None] * C + jnp.einsum('bhs,bhsd,bhsv->bhdv', wa, kb, vb)
        n_new = decay[..., None] * n + jnp.einsum('bhs,bhsd->bhd', wa, kb)
        return (C_new, n_new, m_new), h

    init = (jnp.zeros((B, NH, DK, DV), jnp.float32),
            jnp.zeros((B, NH, DK), jnp.float32),
            jnp.zeros((B, NH), jnp.float32))
    _, hc = lax.scan(step, init, (qc, kc, vc, ic, fc))
    hs = hc.transpose(1, 0, 3, 2, 4).reshape(B, S, NH, DV)
    hs = hs * lax.rsqrt(jnp.mean(hs * hs, axis=-1, keepdims=True) + NORM_EPS)
    hs = hs.reshape(B, S, NH * DV) * head_norm.astype(jnp.float32)
    out = (hs * jax.nn.sigmoid(og)).astype(xn.dtype)
    return out @ w_out


def stick_breaking(q, k, v):
    B, S, H, Dh = q.shape
    scale = Dh ** -0.5
    outs = []
    for blk in range(S // SB_BLOCK):
        t0 = blk * SB_BLOCK
        t1 = t0 + SB_BLOCK
        qb = q[:, t0:t1]
        kb = k[:, :t1]
        vb = v[:, :t1]
        z = jnp.einsum('bqhd,bkhd->bhqk', qb, kb).astype(jnp.float32) * scale
        qpos = t0 + jnp.arange(SB_BLOCK)[:, None]
        kpos = jnp.arange(t1)[None, :]
        mask = kpos < qpos
        log_1mb = jnp.where(mask, jax.nn.log_sigmoid(-z), 0.0)
        suffix = lax.cumsum(log_1mb, axis=3, reverse=True) - log_1mb
        A = jnp.where(mask, jnp.exp(jax.nn.log_sigmoid(z) + suffix), 0.0)
        outs.append(jnp.einsum('bhqk,bkhd->bqhd', A.astype(v.dtype), vb))
    return jnp.concatenate(outs, axis=1)


def conv_ffn(xn, w_up, conv_w, conv_b, w_down):
    S = xn.shape[1]
    u = xn @ w_up
    up = jnp.pad(u, ((0, 0), (CONV_WIDTH - 1, 0), (0, 0)))
    c = (conv_w[0] * up[:, 0:S] + conv_w[1] * up[:, 1:S + 1]
         + conv_w[2] * up[:, 2:S + 2] + conv_b)
    gate, val = jnp.split(c, 2, axis=-1)
    return (jax.nn.silu(gate) * val) @ w_down


def per_layer_embed(h, p_l, w_pe, g_norm, w_gate):
    gate = jax.nn.sigmoid((rmsnorm(h, g_norm) @ w_gate).astype(jnp.float32))
    return ((p_l @ w_pe).astype(jnp.float32) * gate).astype(h.dtype)


def setup_inputs(seed: int = 0) -> dict:
    key = jax.random.key(seed)
    ks = iter(jax.random.split(key, 32))
    f32 = jnp.float32

    def nrm(shape, scale):
        return jax.random.normal(next(ks), shape, f32) * scale

    def gain(shape):
        return 1.0 + 0.02 * jax.random.normal(next(ks), shape, f32)

    D, F, NA, NB, L = D_MODEL, FFN_DIM, N_A_LAYERS, N_B_LAYERS, DEPTH
    gate_bias = jnp.concatenate(
        [-2.0 + 0.1 * jax.random.normal(next(ks), (NA, MLSTM_HEADS), f32),
         3.0 + 0.5 * jax.random.normal(next(ks), (NA, MLSTM_HEADS), f32)], axis=-1)
    return {
        "x": nrm((BATCH, SEQ, D), 1.0),
        "p": nrm((DEPTH, BATCH, SEQ, PLE_DIM), 1.0),
        "a_norm_pre": gain((NA, D)),
        "a_w_in": nrm((NA, D, A_IN_W), D ** -0.5),
        "a_gate_bias": gate_bias,
        "a_head_norm": gain((NA, V_W)),
        "a_w_out": nrm((NA, V_W, D), V_W ** -0.5),
        "a_norm_post": gain((NA, D)),
        "kv_norm": gain((D,)),
        "kv_w": nrm((D, 2 * SB_HEADS * SB_HEAD_DIM), D ** -0.5),
        "b_norm_pre": gain((NB, D)),
        "b_w_q": nrm((NB, D, SB_HEADS * SB_HEAD_DIM), D ** -0.5),
        "b_w_out": nrm((NB, SB_HEADS * SB_HEAD_DIM, D), D ** -0.5),
        "b_norm_post": gain((NB, D)),
        "f_norm_pre": gain((L, D)),
        "f_w_up": nrm((L, D, 2 * F), D ** -0.5),
        "f_conv_w": nrm((L, CONV_WIDTH, 2 * F), CONV_WIDTH ** -0.5),
        "f_conv_b": nrm((L, 2 * F), 0.01),
        "f_w_down": nrm((L, F, D), F ** -0.5),
        "f_norm_post": gain((L, D)),
        "ple_w": nrm((L, PLE_DIM, D), PLE_DIM ** -0.5),
        "ple_gate_norm": gain((L, D)),
        "ple_gate_w": nrm((L, D, D), D ** -0.5),
    }


def reference(x, p, a_norm_pre, a_w_in, a_gate_bias, a_head_norm, a_w_out, a_norm_post,
              kv_norm, kv_w, b_norm_pre, b_w_q, b_w_out, b_norm_post,
              f_norm_pre, f_w_up, f_conv_w, f_conv_b, f_w_down, f_norm_post,
              ple_w, ple_gate_norm, ple_gate_w):
    B, S, D = x.shape
    H, Dh = SB_HEADS, SB_HEAD_DIM
    h = x
    k_sh = None
    v_sh = None
    for layer in range(DEPTH):
        if layer < N_A_LAYERS:
            a = layer
            mix = mlstm(rmsnorm(h, a_norm_pre[a]), a_w_in[a], a_gate_bias[a], a_head_norm[a], a_w_out[a])
            h = h + rmsnorm(mix, a_norm_post[a])
        else:
            if layer == N_A_LAYERS:
                kv = rmsnorm(h, kv_norm) @ kv_w
                k_sh = kv[..., :H * Dh].reshape(B, S, H, Dh)
                v_sh = kv[..., H * Dh:].reshape(B, S, H, Dh)
            j = layer - N_A_LAYERS
            q = (rmsnorm(h, b_norm_pre[j]) @ b_w_q[j]).reshape(B, S, H, Dh)
            mix = stick_breaking(q, k_sh, v_sh).reshape(B, S, H * Dh) @ b_w_out[j]
            h = h + rmsnorm(mix, b_norm_post[j])
        ffn = conv_ffn(rmsnorm(h, f_norm_pre[layer]), f_w_up[layer], f_conv_w[layer],
                       f_conv_b[layer], f_w_down[layer])
        h = h + rmsnorm(ffn, f_norm_post[layer])
        h = h + per_layer_embed(h, p[layer], ple_w[layer], ple_gate_norm[layer], ple_gate_w[layer])
    return h
```

```python
import functools

import jax
import jax.numpy as jnp
from jax import lax
from jax.experimental import pallas as pl
from jax.experimental.pallas import tpu as pltpu

NORM_EPS = 1e-6
GATE_SOFTCAP = 15.0
SB_HEAD_DIM = 128
MLSTM_CHUNK = 256
CONV_HALO = 16
LANES = 128
VMEM_LIMIT_BYTES = 56 * 1024 * 1024

TM = 512
TN = 512
TK = 512
ATT_BLOCK = 128

BF16 = jnp.bfloat16
F32 = jnp.float32


def _cparams(n_axes):
    return pltpu.CompilerParams(dimension_semantics=("arbitrary",) * n_axes,
                                vmem_limit_bytes=VMEM_LIMIT_BYTES)


def _rms(x, g):
    return x * lax.rsqrt(jnp.mean(x * x, axis=-1, keepdims=True) + NORM_EPS) * g


def _dot(a, b):
    return jnp.dot(a, b, preferred_element_type=F32)


def _sigmoid(x):
    return 1.0 / (1.0 + jnp.exp(-x))


def _log_sigmoid(x):
    return jnp.minimum(x, 0.0) - jnp.log(1.0 + jnp.exp(-jnp.abs(x)))


def _softcap(x):
    return GATE_SOFTCAP * jnp.tanh(x / GATE_SOFTCAP)


def _nmm_kernel(x_ref, g_ref, w_ref, o_ref, xn_ref, *, scale):
    @pl.when(pl.program_id(1) == 0)
    def _():
        xn_ref[...] = _rms(x_ref[...], g_ref[...]).astype(BF16)

    y = _dot(xn_ref[...], w_ref[...])
    if scale != 1.0:
        y = y * scale
    o_ref[...] = y.astype(o_ref.dtype)


def norm_matmul(x, g, w, out_dtype, scale=1.0):
    T, D = x.shape
    N = w.shape[1]
    tm, tn = min(TM, T), min(TN, N)
    return pl.pallas_call(
        functools.partial(_nmm_kernel, scale=scale),
        out_shape=jax.ShapeDtypeStruct((T, N), out_dtype),
        grid=(T // tm, N // tn),
        in_specs=[pl.BlockSpec((tm, D), lambda i, j: (i, 0)),
                  pl.BlockSpec((1, D), lambda i, j: (0, 0)),
                  pl.BlockSpec((D, tn), lambda i, j: (0, j))],
        out_specs=pl.BlockSpec((tm, tn), lambda i, j: (i, j)),
        scratch_shapes=[pltpu.VMEM((tm, D), BF16)],
        compiler_params=_cparams(2),
        name="norm_matmul",
    )(x, g, w)


def _ffn_up_kernel(x_ref, xh_ref, g_ref, wg_ref, wv_ref, cwg_ref, cwv_ref, cbg_ref, cbv_ref,
                   o_ref, xn_ref, ug_ref, uv_ref, *, tm, tiles_per_seq):
    i = pl.program_id(0)

    @pl.when(pl.program_id(1) == 0)
    def _():
        g = g_ref[...]
        xn_ref[CONV_HALO:, :] = _rms(x_ref[...], g).astype(BF16)
        keep = (i % tiles_per_seq != 0).astype(F32)
        xn_ref[:CONV_HALO, :] = (_rms(xh_ref[...], g) * keep).astype(BF16)

    xn = xn_ref[...]
    ug_ref[...] = _dot(xn, wg_ref[...])
    uv_ref[...] = _dot(xn, wv_ref[...])

    def conv(u_ref, cw_ref, cb_ref):
        cw = cw_ref[...]
        h0 = CONV_HALO
        return (cw[0:1] * u_ref[h0 - 2:h0 - 2 + tm, :] + cw[1:2] * u_ref[h0 - 1:h0 - 1 + tm, :]
                + cw[2:3] * u_ref[h0:h0 + tm, :] + cb_ref[...])

    gate = conv(ug_ref, cwg_ref, cbg_ref)
    val = conv(uv_ref, cwv_ref, cbv_ref)
    o_ref[...] = (gate * _sigmoid(gate) * val).astype(o_ref.dtype)


def ffn_up(x, g, wg, wv, cwg, cwv, cbg, cbv, seq_len):
    T, D = x.shape
    Fp = wg.shape[1]
    tm, tn = min(TM, T), min(TN, Fp)
    hb = tm // CONV_HALO
    kern = functools.partial(_ffn_up_kernel, tm=tm, tiles_per_seq=seq_len // tm)
    wspec = pl.BlockSpec((D, tn), lambda i, j: (0, j))
    cwspec = pl.BlockSpec((3, tn), lambda i, j: (0, j))
    cbspec = pl.BlockSpec((1, tn), lambda i, j: (0, j))
    return pl.pallas_call(
        kern,
        out_shape=jax.ShapeDtypeStruct((T, Fp), BF16),
        grid=(T // tm, Fp // tn),
        in_specs=[pl.BlockSpec((tm, D), lambda i, j: (i, 0)),
                  pl.BlockSpec((CONV_HALO, D), lambda i, j: (jnp.maximum(i * hb - 1, 0), 0)),
                  pl.BlockSpec((1, D), lambda i, j: (0, 0)),
                  wspec, wspec, cwspec, cwspec, cbspec, cbspec],
        out_specs=pl.BlockSpec((tm, tn), lambda i, j: (i, j)),
        scratch_shapes=[pltpu.VMEM((tm + CONV_HALO, D), BF16),
                        pltpu.VMEM((tm + CONV_HALO, tn), F32),
                        pltpu.VMEM((tm + CONV_HALO, tn), F32)],
        compiler_params=_cparams(2),
        name="ffn_up",
    )(x, x, g, wg, wv, cwg, cwv, cbg, cbv)


def _mm_norm_res_kernel(a_ref, w_ref, g_ref, h_ref, o_ref, *, nk, cn):
    k = pl.program_id(1)
    tm, D = o_ref.shape
    chunks = [slice(n * cn, (n + 1) * cn) for n in range(D // cn)]

    @pl.when(k == 0)
    def _():
        o_ref[...] = jnp.zeros_like(o_ref)

    a = a_ref[...]
    for sl in chunks:
        o_ref[:, sl] += _dot(a, w_ref[:, sl])

    @pl.when(k == nk - 1)
    def _():
        ss = jnp.zeros((tm, 1), F32)
        for sl in chunks:
            y = o_ref[:, sl]
            ss = ss + jnp.sum(y * y, axis=-1, keepdims=True)
        inv = lax.rsqrt(ss / D + NORM_EPS)
        for sl in chunks:
            o_ref[:, sl] = h_ref[:, sl] + o_ref[:, sl] * inv * g_ref[:, sl]


def matmul_norm_residual(a, w, g, h):
    T, K = a.shape
    D = w.shape[1]
    tm, tk = min(TM, T), min(TK, K)
    nk = K // tk
    return pl.pallas_call(
        functools.partial(_mm_norm_res_kernel, nk=nk, cn=min(TN, D)),
        out_shape=jax.ShapeDtypeStruct((T, D), F32),
        grid=(T // tm, nk),
        in_specs=[pl.BlockSpec((tm, tk), lambda i, k: (i, k)),
                  pl.BlockSpec((tk, D), lambda i, k: (k, 0)),
                  pl.BlockSpec((1, D), lambda i, k: (0, 0)),
                  pl.BlockSpec((tm, D), lambda i, k: (i, 0))],
        out_specs=pl.BlockSpec((tm, D), lambda i, k: (i, 0)),
        compiler_params=_cparams(2),
        name="matmul_norm_residual",
    )(a, w, g, h)


def _ple_kernel(h_ref, g_ref, wg_ref, p_ref, wpe_ref, o_ref, xn_ref, *, tn):
    j = pl.program_id(1)

    @pl.when(j == 0)
    def _():
        xn_ref[...] = _rms(h_ref[...], g_ref[...]).astype(BF16)

    gate = _sigmoid(_dot(xn_ref[...], wg_ref[...]))
    pe = _dot(p_ref[...].astype(BF16), wpe_ref[...])
    col = pl.multiple_of(j * tn, tn)
    o_ref[...] = h_ref[:, pl.ds(col, tn)] + pe * gate


def per_layer_embed(h, g, wg, p, wpe):
    T, D = h.shape
    P = p.shape[1]
    tm, tn = min(TM, T), min(TN, D)
    return pl.pallas_call(
        functools.partial(_ple_kernel, tn=tn),
        out_shape=jax.ShapeDtypeStruct((T, D), F32),
        grid=(T // tm, D // tn),
        in_specs=[pl.BlockSpec((tm, D), lambda i, j: (i, 0)),
                  pl.BlockSpec((1, D), lambda i, j: (0, 0)),
                  pl.BlockSpec((D, tn), lambda i, j: (0, j)),
                  pl.BlockSpec((tm, P), lambda i, j: (i, 0)),
                  pl.BlockSpec((P, tn), lambda i, j: (0, j))],
        out_specs=pl.BlockSpec((tm, tn), lambda i, j: (i, j)),
        scratch_shapes=[pltpu.VMEM((tm, D), BF16)],
        compiler_params=_cparams(2),
        name="per_layer_embed",
    )(h, g, wg, p, wpe)


def _mlstm_kernel(gb_ref, q_ref, k_ref, v_ref, og_ref, gcol_ref, irow_ref, frow_ref, hn_ref,
                  o_ref, c_ref, n_ref, m_ref, *, nh, L, dk):
    head = pl.program_id(0) % nh
    c = pl.program_id(1)

    @pl.when(c == 0)
    def _():
        c_ref[...] = jnp.zeros_like(c_ref)
        n_ref[...] = jnp.zeros_like(n_ref)
        m_ref[...] = jnp.zeros_like(m_ref)

    bias_i = gb_ref[head]
    bias_f = gb_ref[nh + head]

    i_row = _softcap(irow_ref[pl.ds(c, 1), :] + bias_i)
    f_row = _log_sigmoid(_softcap(frow_ref[pl.ds(c, 1), :] + bias_f))
    gcol = gcol_ref[...]
    lane = lax.broadcasted_iota(jnp.int32, gcol.shape, 1)
    i_col = _softcap(jnp.sum(jnp.where(lane == head, gcol, 0.0), axis=1, keepdims=True) + bias_i)
    f_col = _log_sigmoid(_softcap(
        jnp.sum(jnp.where(lane == nh + head, gcol, 0.0), axis=1, keepdims=True) + bias_f))

    t_idx = lax.broadcasted_iota(jnp.int32, (L, L), 0)
    s_idx = lax.broadcasted_iota(jnp.int32, (L, L), 1)
    causal = s_idx <= t_idx
    b_col = jnp.sum(jnp.where(causal, f_row, 0.0), axis=1, keepdims=True)
    b_row = jnp.sum(jnp.where(t_idx <= s_idx, f_col, 0.0), axis=0, keepdims=True)
    g_tot = jnp.sum(f_row, axis=1, keepdims=True)

    m_prev = m_ref[...]
    dmat = jnp.where(causal, b_col - b_row + i_row, -jnp.inf)
    m_inter = b_col + m_prev
    m_t = jnp.maximum(m_inter, jnp.max(dmat, axis=1, keepdims=True))
    w_intra = jnp.exp(dmat - m_t)
    w_inter = jnp.exp(m_inter - m_t)

    q = q_ref[...]
    k = k_ref[...]
    v = v_ref[...]
    qk_scale = dk ** -0.5
    s_qk = lax.dot_general(q, k, (((1,), (1,)), ((), ())), preferred_element_type=F32)
    p_mat = s_qk * qk_scale * w_intra
    c_state = c_ref[...]
    n_state = n_ref[...]
    num = w_inter * (_dot(q, c_state.astype(BF16)) * qk_scale) + _dot(p_mat.astype(BF16), v)
    qn = jnp.sum(q.astype(F32) * n_state, axis=1, keepdims=True) * qk_scale
    den = w_inter * qn + jnp.sum(p_mat, axis=1, keepdims=True)
    hs = num / jnp.maximum(jnp.abs(den), jnp.exp(-m_t))

    a_col = g_tot - b_col + i_col
    a_row = g_tot - b_row + i_row
    m_new = jnp.maximum(g_tot + m_prev, jnp.max(a_row, axis=1, keepdims=True))
    wa_col = jnp.exp(a_col - m_new)
    decay = jnp.exp(g_tot + m_prev - m_new)
    wv = (wa_col * v.astype(F32)).astype(BF16)
    c_ref[...] = decay * c_state + lax.dot_general(k, wv, (((0,), (0,)), ((), ())),
                                                   preferred_element_type=F32)
    n_ref[...] = decay * n_state + jnp.sum(wa_col * k.astype(F32), axis=0, keepdims=True)
    m_ref[...] = m_new

    hs = hs * lax.rsqrt(jnp.mean(hs * hs, axis=-1, keepdims=True) + NORM_EPS)
    o_ref[...] = (hs * hn_ref[...] * _sigmoid(og_ref[...].astype(F32))).astype(o_ref.dtype)


def mlstm_core(proj, gates, gate_bias, head_norm, batch, seq_len, nh, dk, dv):
    T = proj.shape[0]
    L = min(MLSTM_CHUNK, seq_len)
    nc = seq_len // L
    g2 = gates[:, :2 * nh]
    g_rows = g2.reshape(batch, nc, L, 2 * nh).transpose(0, 3, 1, 2)
    k_off = (nh * dk) // dk
    v_off = (2 * nh * dk) // dv
    og_off = (2 * nh * dk + nh * dv) // dv
    row = lambda bh, c: (bh // nh) * nc + c
    kern = functools.partial(_mlstm_kernel, nh=nh, L=L, dk=dk)
    return pl.pallas_call(
        kern,
        out_shape=jax.ShapeDtypeStruct((T, nh * dv), BF16),
        grid=(batch * nh, nc),
        in_specs=[pl.BlockSpec(memory_space=pltpu.SMEM),
                  pl.BlockSpec((L, dk), lambda bh, c: (row(bh, c), bh % nh)),
                  pl.BlockSpec((L, dk), lambda bh, c: (row(bh, c), k_off + bh % nh)),
                  pl.BlockSpec((L, dv), lambda bh, c: (row(bh, c), v_off + bh % nh)),
                  pl.BlockSpec((L, dv), lambda bh, c: (row(bh, c), og_off + bh % nh)),
                  pl.BlockSpec((L, 2 * nh), lambda bh, c: (row(bh, c), 0)),
                  pl.BlockSpec((None, None, nc, L), lambda bh, c: (bh // nh, bh % nh, 0, 0)),
                  pl.BlockSpec((None, None, nc, L), lambda bh, c: (bh // nh, nh + bh % nh, 0, 0)),
                  pl.BlockSpec((1, dv), lambda bh, c: (0, bh % nh))],
        out_specs=pl.BlockSpec((L, dv), lambda bh, c: (row(bh, c), bh % nh)),
        scratch_shapes=[pltpu.VMEM((dk, dv), F32), pltpu.VMEM((1, dk), F32), pltpu.VMEM((1, 1), F32)],
        compiler_params=_cparams(2),
        name="mlstm_core",
    )(gate_bias, proj, proj, proj, proj, g2, g_rows, g_rows, head_norm)


def _sb_kernel(q_ref, k_ref, v_ref, o_ref, *, blk):
    qi = pl.program_id(1)
    q = q_ref[...]
    row = lax.broadcasted_iota(jnp.int32, (blk, blk), 0)
    col = lax.broadcasted_iota(jnp.int32, (blk, blk), 1)
    upper = (row > col).astype(BF16)

    def body(it, carry):
        acc, run = carry
        kb = qi - it
        start = pl.multiple_of(kb * blk, blk)
        kt = k_ref[pl.ds(start, blk), :]
        vt = v_ref[pl.ds(start, blk), :]
        z = lax.dot_general(q, kt, (((1,), (1,)), ((), ())), preferred_element_type=F32)
        mask = (kb * blk + col) < (qi * blk + row)
        sp = jnp.maximum(z, 0.0) + jnp.log(1.0 + jnp.exp(-jnp.abs(z)))
        l1m = jnp.where(mask, -sp, 0.0)
        lb = z - sp
        hi = l1m.astype(BF16)
        lo = (l1m - hi.astype(F32)).astype(BF16)
        suffix = _dot(hi, upper) + _dot(lo, upper)
        a = jnp.where(mask, jnp.exp(lb + suffix + run), 0.0)
        acc = acc + _dot(a.astype(BF16), vt)
        run = run + jnp.sum(l1m, axis=1, keepdims=True)
        return acc, run

    dh = q.shape[1]
    acc, _ = lax.fori_loop(0, qi + 1, body,
                           (jnp.zeros((blk, dh), F32), jnp.zeros((blk, 1), F32)))
    o_ref[...] = acc.astype(o_ref.dtype)


def stick_breaking(q, kv, batch, seq_len, heads):
    T = q.shape[0]
    dh = SB_HEAD_DIM
    blk = min(ATT_BLOCK, seq_len)
    nq = seq_len // blk
    return pl.pallas_call(
        functools.partial(_sb_kernel, blk=blk),
        out_shape=jax.ShapeDtypeStruct((T, heads * dh), BF16),
        grid=(batch * heads, nq),
        in_specs=[pl.BlockSpec((blk, dh), lambda g, i: ((g // heads) * nq + i, g % heads)),
                  pl.BlockSpec((seq_len, dh), lambda g, i: (g // heads, g % heads)),
                  pl.BlockSpec((seq_len, dh), lambda g, i: (g // heads, heads + g % heads))],
        out_specs=pl.BlockSpec((blk, dh), lambda g, i: ((g // heads) * nq + i, g % heads)),
        compiler_params=_cparams(2),
        name="stick_breaking",
    )(q, kv, kv)


def _pad_cols(a, n):
    return jnp.pad(a, ((0, 0), (0, n - a.shape[1])))


def kernel(x, p, a_norm_pre, a_w_in, a_gate_bias, a_head_norm, a_w_out, a_norm_post, kv_norm, kv_w, b_norm_pre, b_w_q, b_w_out, b_norm_post, f_norm_pre, f_w_up, f_conv_w, f_conv_b, f_w_down, f_norm_post, ple_w, ple_gate_norm, ple_gate_w):
    B, S, D = x.shape
    T = B * S
    depth = p.shape[0]
    n_a = a_w_in.shape[0]
    nh = a_gate_bias.shape[1] // 2
    v_w = a_w_out.shape[1]
    dv = v_w // nh
    qk_w = (a_w_in.shape[2] - 2 * v_w - 2 * nh) // 2
    dk = qk_w // nh
    sb_heads = b_w_q.shape[2] // SB_HEAD_DIM
    F = f_w_down.shape[1]
    Fp = -(-F // TN) * TN
    row = lambda g: g.reshape(1, -1)

    h = x.reshape(T, D)
    kv = None
    for layer in range(depth):
        if layer < n_a:
            a = layer
            w_in = a_w_in[a]
            n_main = 2 * qk_w + 2 * v_w
            proj = norm_matmul(h, row(a_norm_pre[a]), w_in[:, :n_main].astype(BF16), BF16)
            w_gates = _pad_cols(w_in[:, n_main:], LANES).astype(BF16)
            gates = norm_matmul(h, row(a_norm_pre[a]), w_gates, F32)
            mixed = mlstm_core(proj, gates, a_gate_bias[a], row(a_head_norm[a]), B, S, nh, dk, dv)
            h = matmul_norm_residual(mixed, a_w_out[a].astype(BF16), row(a_norm_post[a]), h)
        else:
            j = layer - n_a
            if kv is None:
                kv = norm_matmul(h, row(kv_norm), kv_w.astype(BF16), BF16)
            q = norm_matmul(h, row(b_norm_pre[j]), b_w_q[j].astype(BF16), BF16,
                            scale=SB_HEAD_DIM ** -0.5)
            mixed = stick_breaking(q, kv, B, S, sb_heads)
            h = matmul_norm_residual(mixed, b_w_out[j].astype(BF16), row(b_norm_post[j]), h)

        w_up = f_w_up[layer]
        wg = _pad_cols(w_up[:, :F], Fp).astype(BF16)
        wv = _pad_cols(w_up[:, F:], Fp).astype(BF16)
        cw = f_conv_w[layer]
        cb = row(f_conv_b[layer])
        act = ffn_up(h, row(f_norm_pre[layer]), wg, wv,
                     _pad_cols(cw[:, :F], Fp), _pad_cols(cw[:, F:], Fp),
                     _pad_cols(cb[:, :F], Fp), _pad_cols(cb[:, F:], Fp), S)
        w_down = jnp.pad(f_w_down[layer], ((0, Fp - F), (0, 0))).astype(BF16)
        h = matmul_norm_residual(act, w_down, row(f_norm_post[layer]), h)
        h = per_layer_embed(h, row(ple_gate_norm[layer]), ple_gate_w[layer].astype(BF16),
                            p[layer].reshape(T, -1), ple_w[layer].astype(BF16))
    return h.reshape(B, S, D)
```

```python
import functools

import jax
import jax.numpy as jnp
from jax import lax
from jax.experimental import pallas as pl
from jax.experimental.pallas import tpu as pltpu

NORM_EPS = 1e-6
GATE_SOFTCAP = 15.0
SB_HEAD_DIM = 128
MLSTM_CHUNK = 256
CONV_HALO = 16
LANES = 128
VMEM_LIMIT_BYTES = 56 * 1024 * 1024

TM = 512
TN = 512
TK = 512
ATT_BLOCK = 256
ATT_HEAD_GROUP = 4

BF16 = jnp.bfloat16
F32 = jnp.float32


def _cparams(n_axes):
    return pltpu.CompilerParams(dimension_semantics=("arbitrary",) * n_axes,
                                vmem_limit_bytes=VMEM_LIMIT_BYTES)


def _rms(x, g):
    return x * lax.rsqrt(jnp.mean(x * x, axis=-1, keepdims=True) + NORM_EPS) * g


def _dot(a, b):
    return jnp.dot(a, b, preferred_element_type=F32)


def _sigmoid(x):
    return 1.0 / (1.0 + jnp.exp(-x))


def _log_sigmoid(x):
    return jnp.minimum(x, 0.0) - jnp.log(1.0 + jnp.exp(-jnp.abs(x)))


def _softcap(x):
    return GATE_SOFTCAP * jnp.tanh(x / GATE_SOFTCAP)


def _nmm_kernel(x_ref, g_ref, w_ref, o_ref, xn_ref, *, scale):
    @pl.when(pl.program_id(1) == 0)
    def _():
        xn_ref[...] = _rms(x_ref[...], g_ref[...]).astype(BF16)

    y = _dot(xn_ref[...], w_ref[...])
    if scale != 1.0:
        y = y * scale
    o_ref[...] = y.astype(o_ref.dtype)


def norm_matmul(x, g, w, out_dtype, scale=1.0):
    T, D = x.shape
    N = w.shape[1]
    tm, tn = min(TM, T), min(TN, N)
    return pl.pallas_call(
        functools.partial(_nmm_kernel, scale=scale),
        out_shape=jax.ShapeDtypeStruct((T, N), out_dtype),
        grid=(T // tm, N // tn),
        in_specs=[pl.BlockSpec((tm, D), lambda i, j: (i, 0)),
                  pl.BlockSpec((1, D), lambda i, j: (0, 0)),
                  pl.BlockSpec((D, tn), lambda i, j: (0, j))],
        out_specs=pl.BlockSpec((tm, tn), lambda i, j: (i, j)),
        scratch_shapes=[pltpu.VMEM((tm, D), BF16)],
        compiler_params=_cparams(2),
        name="norm_matmul",
    )(x, g, w)


def _ffn_up_kernel(x_ref, xh_ref, g_ref, wg_ref, wv_ref, cwg_ref, cwv_ref, cbg_ref, cbv_ref,
                   o_ref, xn_ref, ug_ref, uv_ref, *, tm, tiles_per_seq):
    i = pl.program_id(0)

    @pl.when(pl.program_id(1) == 0)
    def _():
        g = g_ref[...]
        xn_ref[CONV_HALO:, :] = _rms(x_ref[...], g).astype(BF16)
        keep = (i % tiles_per_seq != 0).astype(F32)
        xn_ref[:CONV_HALO, :] = (_rms(xh_ref[...], g) * keep).astype(BF16)

    xn = xn_ref[...]
    ug_ref[...] = _dot(xn, wg_ref[...])
    uv_ref[...] = _dot(xn, wv_ref[...])

    def conv(u_ref, cw_ref, cb_ref):
        cw = cw_ref[...]
        h0 = CONV_HALO
        return (cw[0:1] * u_ref[h0 - 2:h0 - 2 + tm, :] + cw[1:2] * u_ref[h0 - 1:h0 - 1 + tm, :]
                + cw[2:3] * u_ref[h0:h0 + tm, :] + cb_ref[...])

    gate = conv(ug_ref, cwg_ref, cbg_ref)
    val = conv(uv_ref, cwv_ref, cbv_ref)
    o_ref[...] = (gate * _sigmoid(gate) * val).astype(o_ref.dtype)


def ffn_up(x, g, w, cw, cb, seq_len):
    T, D = x.shape
    Fp = w.shape[1] // 2
    tm, tn = min(TM, T), min(TN, Fp)
    hb = tm // CONV_HALO
    nj = Fp // tn
    kern = functools.partial(_ffn_up_kernel, tm=tm, tiles_per_seq=seq_len // tm)
    halves = lambda rows: [pl.BlockSpec((rows, tn), lambda i, j: (0, j)),
                           pl.BlockSpec((rows, tn), lambda i, j: (0, nj + j))]
    return pl.pallas_call(
        kern,
        out_shape=jax.ShapeDtypeStruct((T, Fp), BF16),
        grid=(T // tm, nj),
        in_specs=[pl.BlockSpec((tm, D), lambda i, j: (i, 0)),
                  pl.BlockSpec((CONV_HALO, D), lambda i, j: (jnp.maximum(i * hb - 1, 0), 0)),
                  pl.BlockSpec((1, D), lambda i, j: (0, 0))] + halves(D) + halves(3) + halves(1),
        out_specs=pl.BlockSpec((tm, tn), lambda i, j: (i, j)),
        scratch_shapes=[pltpu.VMEM((tm + CONV_HALO, D), BF16),
                        pltpu.VMEM((tm + CONV_HALO, tn), F32),
                        pltpu.VMEM((tm + CONV_HALO, tn), F32)],
        compiler_params=_cparams(2),
        name="ffn_up",
    )(x, x, g, w, w, cw, cw, cb, cb)


def _mm_norm_res_kernel(a_ref, w_ref, g_ref, h_ref, o_ref, *, nk, cn):
    k = pl.program_id(1)
    tm, D = o_ref.shape
    chunks = [slice(n * cn, (n + 1) * cn) for n in range(D // cn)]

    @pl.when(k == 0)
    def _():
        o_ref[...] = jnp.zeros_like(o_ref)

    a = a_ref[...]
    for sl in chunks:
        o_ref[:, sl] += _dot(a, w_ref[:, sl])

    @pl.when(k == nk - 1)
    def _():
        ss = jnp.zeros((tm, 1), F32)
        for sl in chunks:
            y = o_ref[:, sl]
            ss = ss + jnp.sum(y * y, axis=-1, keepdims=True)
        inv = lax.rsqrt(ss / D + NORM_EPS)
        for sl in chunks:
            o_ref[:, sl] = h_ref[:, sl] + o_ref[:, sl] * inv * g_ref[:, sl]


def matmul_norm_residual(a, w, g, h):
    T, K = a.shape
    D = w.shape[1]
    tm, tk = min(TM, T), min(TK, K)
    nk = K // tk
    return pl.pallas_call(
        functools.partial(_mm_norm_res_kernel, nk=nk, cn=min(TN, D)),
        out_shape=jax.ShapeDtypeStruct((T, D), F32),
        grid=(T // tm, nk),
        in_specs=[pl.BlockSpec((tm, tk), lambda i, k: (i, k)),
                  pl.BlockSpec((tk, D), lambda i, k: (k, 0)),
                  pl.BlockSpec((1, D), lambda i, k: (0, 0)),
                  pl.BlockSpec((tm, D), lambda i, k: (i, 0))],
        out_specs=pl.BlockSpec((tm, D), lambda i, k: (i, 0)),
        compiler_params=_cparams(2),
        name="matmul_norm_residual",
    )(a, w, g, h)


def _ple_kernel(h_ref, g_ref, wg_ref, p_ref, wpe_ref, o_ref, xn_ref, *, tn):
    j = pl.program_id(1)

    @pl.when(j == 0)
    def _():
        xn_ref[...] = _rms(h_ref[...], g_ref[...]).astype(BF16)

    gate = _sigmoid(_dot(xn_ref[...], wg_ref[...]))
    pe = _dot(p_ref[...].astype(BF16), wpe_ref[...])
    col = pl.multiple_of(j * tn, tn)
    o_ref[...] = h_ref[:, pl.ds(col, tn)] + pe * gate


def per_layer_embed(h, g, wg, p, wpe):
    T, D = h.shape
    P = p.shape[1]
    tm, tn = min(TM, T), min(TN, D)
    return pl.pallas_call(
        functools.partial(_ple_kernel, tn=tn),
        out_shape=jax.ShapeDtypeStruct((T, D), F32),
        grid=(T // tm, D // tn),
        in_specs=[pl.BlockSpec((tm, D), lambda i, j: (i, 0)),
                  pl.BlockSpec((1, D), lambda i, j: (0, 0)),
                  pl.BlockSpec((D, tn), lambda i, j: (0, j)),
                  pl.BlockSpec((tm, P), lambda i, j: (i, 0)),
                  pl.BlockSpec((P, tn), lambda i, j: (0, j))],
        out_specs=pl.BlockSpec((tm, tn), lambda i, j: (i, j)),
        scratch_shapes=[pltpu.VMEM((tm, D), BF16)],
        compiler_params=_cparams(2),
        name="per_layer_embed",
    )(h, g, wg, p, wpe)


def _mlstm_kernel(gb_ref, q_ref, k_ref, v_ref, og_ref, gcol_ref, irow_ref, frow_ref, hn_ref,
                  o_ref, c_ref, n_ref, m_ref, *, nh, L, dk):
    head = pl.program_id(0) % nh
    c = pl.program_id(1)

    @pl.when(c == 0)
    def _():
        c_ref[...] = jnp.zeros_like(c_ref)
        n_ref[...] = jnp.zeros_like(n_ref)
        m_ref[...] = jnp.zeros_like(m_ref)

    bias_i = gb_ref[head]
    bias_f = gb_ref[nh + head]

    i_row = _softcap(irow_ref[pl.ds(c, 1), :] + bias_i)
    f_row = _log_sigmoid(_softcap(frow_ref[pl.ds(c, 1), :] + bias_f))
    gcol = gcol_ref[...]
    lane = lax.broadcasted_iota(jnp.int32, gcol.shape, 1)
    i_col = _softcap(jnp.sum(jnp.where(lane == head, gcol, 0.0), axis=1, keepdims=True) + bias_i)
    f_col = _log_sigmoid(_softcap(
        jnp.sum(jnp.where(lane == nh + head, gcol, 0.0), axis=1, keepdims=True) + bias_f))

    t_idx = lax.broadcasted_iota(jnp.int32, (L, L), 0)
    s_idx = lax.broadcasted_iota(jnp.int32, (L, L), 1)
    causal = s_idx <= t_idx
    b_col = jnp.sum(jnp.where(causal, f_row, 0.0), axis=1, keepdims=True)
    b_row = jnp.sum(jnp.where(t_idx <= s_idx, f_col, 0.0), axis=0, keepdims=True)
    g_tot = jnp.sum(f_row, axis=1, keepdims=True)

    m_prev = m_ref[...]
    dmat = jnp.where(causal, b_col - b_row + i_row, -jnp.inf)
    m_inter = b_col + m_prev
    m_t = jnp.maximum(m_inter, jnp.max(dmat, axis=1, keepdims=True))
    w_intra = jnp.exp(dmat - m_t)
    w_inter = jnp.exp(m_inter - m_t)

    q = q_ref[...]
    k = k_ref[...]
    v = v_ref[...]
    qk_scale = dk ** -0.5
    s_qk = lax.dot_general(q, k, (((1,), (1,)), ((), ())), preferred_element_type=F32)
    p_mat = s_qk * qk_scale * w_intra
    c_state = c_ref[...]
    n_state = n_ref[...]
    num = w_inter * (_dot(q, c_state.astype(BF16)) * qk_scale) + _dot(p_mat.astype(BF16), v)
    qn = jnp.sum(q.astype(F32) * n_state, axis=1, keepdims=True) * qk_scale
    den = w_inter * qn + jnp.sum(p_mat, axis=1, keepdims=True)
    hs = num / jnp.maximum(jnp.abs(den), jnp.exp(-m_t))

    a_col = g_tot - b_col + i_col
    a_row = g_tot - b_row + i_row
    m_new = jnp.maximum(g_tot + m_prev, jnp.max(a_row, axis=1, keepdims=True))
    wa_col = jnp.exp(a_col - m_new)
    decay = jnp.exp(g_tot + m_prev - m_new)
    wv = (wa_col * v.astype(F32)).astype(BF16)
    c_ref[...] = decay * c_state + lax.dot_general(k, wv, (((0,), (0,)), ((), ())),
                                                   preferred_element_type=F32)
    n_ref[...] = decay * n_state + jnp.sum(wa_col * k.astype(F32), axis=0, keepdims=True)
    m_ref[...] = m_new

    hs = hs * lax.rsqrt(jnp.mean(hs * hs, axis=-1, keepdims=True) + NORM_EPS)
    o_ref[...] = (hs * hn_ref[...] * _sigmoid(og_ref[...].astype(F32))).astype(o_ref.dtype)


def mlstm_core(proj, gates, gate_bias, head_norm, batch, seq_len, nh, dk, dv):
    T = proj.shape[0]
    L = min(MLSTM_CHUNK, seq_len)
    nc = seq_len // L
    g2 = gates[:, :2 * nh]
    g_rows = g2.reshape(batch, nc, L, 2 * nh).transpose(0, 3, 1, 2)
    k_off = (nh * dk) // dk
    v_off = (2 * nh * dk) // dv
    og_off = (2 * nh * dk + nh * dv) // dv
    row = lambda bh, c: (bh // nh) * nc + c
    kern = functools.partial(_mlstm_kernel, nh=nh, L=L, dk=dk)
    return pl.pallas_call(
        kern,
        out_shape=jax.ShapeDtypeStruct((T, nh * dv), BF16),
        grid=(batch * nh, nc),
        in_specs=[pl.BlockSpec(memory_space=pltpu.SMEM),
                  pl.BlockSpec((L, dk), lambda bh, c: (row(bh, c), bh % nh)),
                  pl.BlockSpec((L, dk), lambda bh, c: (row(bh, c), k_off + bh % nh)),
                  pl.BlockSpec((L, dv), lambda bh, c: (row(bh, c), v_off + bh % nh)),
                  pl.BlockSpec((L, dv), lambda bh, c: (row(bh, c), og_off + bh % nh)),
                  pl.BlockSpec((L, 2 * nh), lambda bh, c: (row(bh, c), 0)),
                  pl.BlockSpec((None, None, nc, L), lambda bh, c: (bh // nh, bh % nh, 0, 0)),
                  pl.BlockSpec((None, None, nc, L), lambda bh, c: (bh // nh, nh + bh % nh, 0, 0)),
                  pl.BlockSpec((1, dv), lambda bh, c: (0, bh % nh))],
        out_specs=pl.BlockSpec((L, dv), lambda bh, c: (row(bh, c), bh % nh)),
        scratch_shapes=[pltpu.VMEM((dk, dv), F32), pltpu.VMEM((1, dk), F32), pltpu.VMEM((1, 1), F32)],
        compiler_params=_cparams(2),
        name="mlstm_core",
    )(gate_bias, proj, proj, proj, proj, g2, g_rows, g_rows, head_norm)


def _sb_kernel(q_ref, k_ref, v_ref, o_ref, acc_ref, run_ref, *, blk, dh, group):
    qi = pl.program_id(1)
    row = lax.broadcasted_iota(jnp.int32, (blk, blk), 0)
    col = lax.broadcasted_iota(jnp.int32, (blk, blk), 1)
    later = (row > col).astype(BF16)
    causal = col < row

    acc_ref[...] = jnp.zeros_like(acc_ref)
    run_ref[...] = jnp.zeros_like(run_ref)

    def tile(kb, masked):
        start = pl.multiple_of(kb * blk, blk)
        for g in range(group):
            cols = slice(g * dh, (g + 1) * dh)
            kt = k_ref[pl.ds(start, blk), cols]
            vt = v_ref[pl.ds(start, blk), cols]
            z = lax.dot_general(q_ref[:, cols], kt, (((1,), (1,)), ((), ())),
                                preferred_element_type=F32)
            log_1mb = jnp.minimum(-z, 0.0) - jnp.log(1.0 + jnp.exp(-jnp.abs(z)))
            if masked:
                log_1mb = jnp.where(causal, log_1mb, 0.0)
            suffix = _dot(log_1mb.astype(BF16), later)
            e = jnp.exp(z + log_1mb + suffix)
            if masked:
                e = jnp.where(causal, e, 0.0)
            run = run_ref[g]
            acc_ref[g] += jnp.exp(run) * _dot(e.astype(BF16), vt)
            run_ref[g] = run + jnp.sum(log_1mb, axis=1, keepdims=True)

    tile(qi, True)

    def body(it, carry):
        tile(qi - 1 - it, False)
        return carry

    lax.fori_loop(0, qi, body, 0)
    for g in range(group):
        o_ref[:, g * dh:(g + 1) * dh] = acc_ref[g].astype(o_ref.dtype)


def stick_breaking(q, kv, batch, seq_len, heads):
    T = q.shape[0]
    dh = SB_HEAD_DIM
    blk = min(ATT_BLOCK, seq_len)
    nq = seq_len // blk
    group = min(ATT_HEAD_GROUP, heads)
    ng = heads // group
    gw = group * dh
    return pl.pallas_call(
        functools.partial(_sb_kernel, blk=blk, dh=dh, group=group),
        out_shape=jax.ShapeDtypeStruct((T, heads * dh), BF16),
        grid=(batch * ng, nq),
        in_specs=[pl.BlockSpec((blk, gw), lambda g, i: ((g // ng) * nq + i, g % ng)),
                  pl.BlockSpec((seq_len, gw), lambda g, i: (g // ng, g % ng)),
                  pl.BlockSpec((seq_len, gw), lambda g, i: (g // ng, ng + g % ng))],
        out_specs=pl.BlockSpec((blk, gw), lambda g, i: ((g // ng) * nq + i, g % ng)),
        scratch_shapes=[pltpu.VMEM((group, blk, dh), F32), pltpu.VMEM((group, blk, 1), F32)],
        compiler_params=_cparams(2),
        name="stick_breaking",
    )(q, kv, kv)


def _pad_cols(a, n):
    return jnp.pad(a, ((0, 0), (0, n - a.shape[1])))


def _split_pad(a, f, fp):
    return jnp.concatenate([_pad_cols(a[:, :f], fp), _pad_cols(a[:, f:], fp)], axis=1)


def kernel(x, p, a_norm_pre, a_w_in, a_gate_bias, a_head_norm, a_w_out, a_norm_post, kv_norm, kv_w, b_norm_pre, b_w_q, b_w_out, b_norm_post, f_norm_pre, f_w_up, f_conv_w, f_conv_b, f_w_down, f_norm_post, ple_w, ple_gate_norm, ple_gate_w):
    B, S, D = x.shape
    T = B * S
    depth = p.shape[0]
    n_a = a_w_in.shape[0]
    nh = a_gate_bias.shape[1] // 2
    v_w = a_w_out.shape[1]
    dv = v_w // nh
    qk_w = (a_w_in.shape[2] - 2 * v_w - 2 * nh) // 2
    dk = qk_w // nh
    sb_heads = b_w_q.shape[2] // SB_HEAD_DIM
    F = f_w_down.shape[1]
    Fp = -(-F // TN) * TN
    row = lambda g: g.reshape(1, -1)

    h = x.reshape(T, D)
    kv = None
    for layer in range(depth):
        if layer < n_a:
            a = layer
            w_in = a_w_in[a]
            n_main = 2 * qk_w + 2 * v_w
            proj = norm_matmul(h, row(a_norm_pre[a]), w_in[:, :n_main].astype(BF16), BF16)
            w_gates = _pad_cols(w_in[:, n_main:], LANES).astype(BF16)
            gates = norm_matmul(h, row(a_norm_pre[a]), w_gates, F32)
            mixed = mlstm_core(proj, gates, a_gate_bias[a], row(a_head_norm[a]), B, S, nh, dk, dv)
            h = matmul_norm_residual(mixed, a_w_out[a].astype(BF16), row(a_norm_post[a]), h)
        else:
            j = layer - n_a
            if kv is None:
                kv = norm_matmul(h, row(kv_norm), kv_w.astype(BF16), BF16)
            q = norm_matmul(h, row(b_norm_pre[j]), b_w_q[j].astype(BF16), BF16,
                            scale=SB_HEAD_DIM ** -0.5)
            mixed = stick_breaking(q, kv, B, S, sb_heads)
            h = matmul_norm_residual(mixed, b_w_out[j].astype(BF16), row(b_norm_post[j]), h)

        act = ffn_up(h, row(f_norm_pre[layer]), _split_pad(f_w_up[layer].astype(BF16), F, Fp),
                     _split_pad(f_conv_w[layer], F, Fp), _split_pad(row(f_conv_b[layer]), F, Fp), S)
        w_down = jnp.pad(f_w_down[layer].astype(BF16), ((0, Fp - F), (0, 0)))
        h = matmul_norm_residual(act, w_down, row(f_norm_post[layer]), h)
        h = per_layer_embed(h, row(ple_gate_norm[layer]), ple_gate_w[layer].astype(BF16),
                            p[layer].reshape(T, -1), ple_w[layer].astype(BF16))
    return h.reshape(B, S, D)
```

```python
import functools
import math

import jax
import jax.numpy as jnp
from jax import lax
from jax.experimental import pallas as pl
from jax.experimental.pallas import tpu as pltpu

NORM_EPS = 1e-6
LOG2_E = math.log2(math.e)
GATE_SOFTCAP = 15.0
SB_HEAD_DIM = 128
MLSTM_CHUNK = 256
CONV_HALO = 16
PREP_ROWS = 1024
PREP_COLS = 640
LANES = 128
VMEM_LIMIT_BYTES = 60 * 1024 * 1024

TM = 512
TN = 1024
TN_FFN = 512
FFN_CHUNK = 512
CN = 512
TK = 1024
ATT_Q_BLOCK = 1024
ATT_K_BLOCK = 256
ATT_HEAD_GROUP = 1

BF16 = jnp.bfloat16
F32 = jnp.float32


def _cparams(n_axes):
    return pltpu.CompilerParams(dimension_semantics=("arbitrary",) * n_axes,
                                vmem_limit_bytes=VMEM_LIMIT_BYTES)


def _rms(x, g):
    return x * lax.rsqrt(jnp.mean(x * x, axis=-1, keepdims=True) + NORM_EPS) * g


def _dot(a, b):
    return jnp.dot(a, b, preferred_element_type=F32)


def _sigmoid(x):
    return 1.0 / (1.0 + jnp.exp(-x))


def _log_sigmoid(x):
    return jnp.minimum(x, 0.0) - jnp.log(1.0 + jnp.exp(-jnp.abs(x)))


def _softcap(x):
    return GATE_SOFTCAP * jnp.tanh(x / GATE_SOFTCAP)


def _nmm_kernel(x_ref, g_ref, w_ref, o_ref, xn_ref, *, scale):
    @pl.when(pl.program_id(1) == 0)
    def _():
        xn_ref[...] = _rms(x_ref[...], g_ref[...]).astype(BF16)

    y = _dot(xn_ref[...], w_ref[...])
    if scale != 1.0:
        y = y * scale
    o_ref[...] = y.astype(o_ref.dtype)


def _layer_spec(block, index_map, layer):
    return pl.BlockSpec((None,) + block, lambda *idx: (layer,) + index_map(*idx))


def norm_matmul(x, g, w, layer, out_dtype, scale=1.0, n_cols=None):
    T, D = x.shape
    N = w.shape[2] if n_cols is None else n_cols
    tm, tn = min(TM, T), min(TN, N)
    return pl.pallas_call(
        functools.partial(_nmm_kernel, scale=scale),
        out_shape=jax.ShapeDtypeStruct((T, N), out_dtype),
        grid=(T // tm, N // tn),
        in_specs=[pl.BlockSpec((tm, D), lambda i, j: (i, 0)),
                  pl.BlockSpec((1, D), lambda i, j: (0, 0)),
                  _layer_spec((D, tn), lambda i, j: (0, j), layer)],
        out_specs=pl.BlockSpec((tm, tn), lambda i, j: (i, j)),
        scratch_shapes=[pltpu.VMEM((tm, D), BF16)],
        compiler_params=_cparams(2),
        name="norm_matmul",
    )(x, g, w)


def _ffn_up_kernel(x_ref, xh_ref, g_ref, wg_ref, wv_ref, cwg_ref, cwv_ref, cbg_ref, cbv_ref,
                   o_ref, xn_ref, ug_ref, uv_ref, *, tm, tiles_per_seq):
    i = pl.program_id(0)

    @pl.when(pl.program_id(1) == 0)
    def _():
        g = g_ref[...]
        xn_ref[CONV_HALO:, :] = _rms(x_ref[...], g).astype(BF16)
        keep = (i % tiles_per_seq != 0).astype(F32)
        xn_ref[:CONV_HALO, :] = (_rms(xh_ref[...], g) * keep).astype(BF16)

    xn = xn_ref[...]
    tn = o_ref.shape[1]
    cn = min(FFN_CHUNK, tn)

    def conv(u_ref, cw_ref, cb_ref, sl):
        cw = cw_ref[:, sl]
        h0 = CONV_HALO
        return (cw[0:1] * u_ref[h0 - 2:h0 - 2 + tm, sl] + cw[1:2] * u_ref[h0 - 1:h0 - 1 + tm, sl]
                + cw[2:3] * u_ref[h0:h0 + tm, sl] + cb_ref[:, sl])

    for n in range(tn // cn):
        sl = slice(n * cn, (n + 1) * cn)
        ug_ref[:, sl] = _dot(xn, wg_ref[:, sl])
        uv_ref[:, sl] = _dot(xn, wv_ref[:, sl])
        gate = conv(ug_ref, cwg_ref, cbg_ref, sl)
        val = conv(uv_ref, cwv_ref, cbv_ref, sl)
        o_ref[:, sl] = (gate * _sigmoid(gate) * val).astype(o_ref.dtype)


def ffn_up(x, g, wg, wv, layer, cw, cb, seq_len):
    T, D = x.shape
    Fp = wg.shape[2]
    tm, tn = min(TM, T), min(TN_FFN, Fp)
    hb = tm // CONV_HALO
    nj = Fp // tn
    kern = functools.partial(_ffn_up_kernel, tm=tm, tiles_per_seq=seq_len // tm)
    halves = lambda rows: [pl.BlockSpec((rows, tn), lambda i, j: (0, j)),
                           pl.BlockSpec((rows, tn), lambda i, j: (0, nj + j))]
    return pl.pallas_call(
        kern,
        out_shape=jax.ShapeDtypeStruct((T, Fp), BF16),
        grid=(T // tm, nj),
        in_specs=[pl.BlockSpec((tm, D), lambda i, j: (i, 0)),
                  pl.BlockSpec((CONV_HALO, D), lambda i, j: (jnp.maximum(i * hb - 1, 0), 0)),
                  pl.BlockSpec((1, D), lambda i, j: (0, 0)),
                  _layer_spec((D, tn), lambda i, j: (0, j), layer),
                  _layer_spec((D, tn), lambda i, j: (0, j), layer)] + halves(3) + halves(1),
        out_specs=pl.BlockSpec((tm, tn), lambda i, j: (i, j)),
        scratch_shapes=[pltpu.VMEM((tm + CONV_HALO, D), BF16),
                        pltpu.VMEM((tm + CONV_HALO, tn), F32),
                        pltpu.VMEM((tm + CONV_HALO, tn), F32)],
        compiler_params=_cparams(2),
        name="ffn_up",
    )(x, x, g, wg, wv, cw, cw, cb, cb)


def _mm_norm_res_kernel(a_ref, w_ref, g_ref, h_ref, o_ref, *, nk, cn):
    k = pl.program_id(1)
    tm, D = o_ref.shape
    chunks = [slice(n * cn, (n + 1) * cn) for n in range(D // cn)]

    @pl.when(k == 0)
    def _():
        o_ref[...] = jnp.zeros_like(o_ref)

    a = a_ref[...]
    for sl in chunks:
        o_ref[:, sl] += _dot(a, w_ref[:, sl])

    @pl.when(k == nk - 1)
    def _():
        ss = jnp.zeros((tm, 1), F32)
        for sl in chunks:
            y = o_ref[:, sl]
            ss = ss + jnp.sum(y * y, axis=-1, keepdims=True)
        inv = lax.rsqrt(ss / D + NORM_EPS)
        for sl in chunks:
            o_ref[:, sl] = h_ref[:, sl] + o_ref[:, sl] * inv * g_ref[:, sl]


def matmul_norm_residual(a, w, layer, g, h):
    T, K = a.shape
    D = w.shape[2]
    tm, tk = min(TM, T), min(TK, K)
    nk = K // tk
    return pl.pallas_call(
        functools.partial(_mm_norm_res_kernel, nk=nk, cn=min(CN, D)),
        out_shape=jax.ShapeDtypeStruct((T, D), F32),
        grid=(T // tm, nk),
        in_specs=[pl.BlockSpec((tm, tk), lambda i, k: (i, k)),
                  _layer_spec((tk, D), lambda i, k: (k, 0), layer),
                  pl.BlockSpec((1, D), lambda i, k: (0, 0)),
                  pl.BlockSpec((tm, D), lambda i, k: (i, 0))],
        out_specs=pl.BlockSpec((tm, D), lambda i, k: (i, 0)),
        compiler_params=_cparams(2),
        name="matmul_norm_residual",
    )(a, w, g, h)


def _ple_kernel(h_ref, g_ref, wg_ref, p_ref, wpe_ref, o_ref, xn_ref, *, tn):
    j = pl.program_id(1)

    @pl.when(j == 0)
    def _():
        xn_ref[...] = _rms(h_ref[...], g_ref[...]).astype(BF16)

    gate = _sigmoid(_dot(xn_ref[...], wg_ref[...]))
    pe = _dot(p_ref[...].astype(BF16), wpe_ref[...])
    col = pl.multiple_of(j * tn, tn)
    o_ref[...] = h_ref[:, pl.ds(col, tn)] + pe * gate


def per_layer_embed(h, g, wg, p, wpe, layer):
    T, D = h.shape
    P = p.shape[2]
    tm, tn = min(TM, T), min(TN, D)
    return pl.pallas_call(
        functools.partial(_ple_kernel, tn=tn),
        out_shape=jax.ShapeDtypeStruct((T, D), F32),
        grid=(T // tm, D // tn),
        in_specs=[pl.BlockSpec((tm, D), lambda i, j: (i, 0)),
                  pl.BlockSpec((1, D), lambda i, j: (0, 0)),
                  _layer_spec((D, tn), lambda i, j: (0, j), layer),
                  _layer_spec((tm, P), lambda i, j: (i, 0), layer),
                  _layer_spec((P, tn), lambda i, j: (0, j), layer)],
        out_specs=pl.BlockSpec((tm, tn), lambda i, j: (i, j)),
        scratch_shapes=[pltpu.VMEM((tm, D), BF16)],
        compiler_params=_cparams(2),
        name="per_layer_embed",
    )(h, g, wg, p, wpe)


def _mlstm_kernel(gb_ref, q_ref, k_ref, v_ref, og_ref, gcol_ref, irow_ref, frow_ref, hn_ref,
                  o_ref, c_ref, n_ref, m_ref, *, nh, L, dk):
    head = pl.program_id(0) % nh
    c = pl.program_id(1)

    @pl.when(c == 0)
    def _():
        c_ref[...] = jnp.zeros_like(c_ref)
        n_ref[...] = jnp.zeros_like(n_ref)
        m_ref[...] = jnp.zeros_like(m_ref)

    bias_i = gb_ref[head]
    bias_f = gb_ref[nh + head]

    i_row = _softcap(irow_ref[pl.ds(c, 1), :] + bias_i)
    f_row = _log_sigmoid(_softcap(frow_ref[pl.ds(c, 1), :] + bias_f))
    gcol = gcol_ref[...]
    lane = lax.broadcasted_iota(jnp.int32, gcol.shape, 1)
    i_col = _softcap(jnp.sum(jnp.where(lane == head, gcol, 0.0), axis=1, keepdims=True) + bias_i)
    f_col = _log_sigmoid(_softcap(
        jnp.sum(jnp.where(lane == nh + head, gcol, 0.0), axis=1, keepdims=True) + bias_f))

    t_idx = lax.broadcasted_iota(jnp.int32, (L, L), 0)
    s_idx = lax.broadcasted_iota(jnp.int32, (L, L), 1)
    causal = s_idx <= t_idx
    b_col = jnp.sum(jnp.where(causal, f_row, 0.0), axis=1, keepdims=True)
    b_row = jnp.sum(jnp.where(t_idx <= s_idx, f_col, 0.0), axis=0, keepdims=True)
    g_tot = jnp.sum(f_row, axis=1, keepdims=True)

    m_prev = m_ref[...]
    dmat = jnp.where(causal, b_col - b_row + i_row, -jnp.inf)
    m_inter = b_col + m_prev
    m_t = jnp.maximum(m_inter, jnp.max(dmat, axis=1, keepdims=True))
    w_intra = jnp.exp(dmat - m_t)
    w_inter = jnp.exp(m_inter - m_t)

    q = q_ref[...]
    k = k_ref[...]
    v = v_ref[...]
    qk_scale = dk ** -0.5
    s_qk = lax.dot_general(q, k, (((1,), (1,)), ((), ())), preferred_element_type=F32)
    p_mat = s_qk * qk_scale * w_intra
    c_state = c_ref[...]
    n_state = n_ref[...]
    num = w_inter * (_dot(q, c_state.astype(BF16)) * qk_scale) + _dot(p_mat.astype(BF16), v)
    qn = jnp.sum(q.astype(F32) * n_state, axis=1, keepdims=True) * qk_scale
    den = w_inter * qn + jnp.sum(p_mat, axis=1, keepdims=True)
    hs = num / jnp.maximum(jnp.abs(den), jnp.exp(-m_t))

    a_col = g_tot - b_col + i_col
    a_row = g_tot - b_row + i_row
    m_new = jnp.maximum(g_tot + m_prev, jnp.max(a_row, axis=1, keepdims=True))
    wa_col = jnp.exp(a_col - m_new)
    decay = jnp.exp(g_tot + m_prev - m_new)
    wv = (wa_col * v.astype(F32)).astype(BF16)
    c_ref[...] = decay * c_state + lax.dot_general(k, wv, (((0,), (0,)), ((), ())),
                                                   preferred_element_type=F32)
    n_ref[...] = decay * n_state + jnp.sum(wa_col * k.astype(F32), axis=0, keepdims=True)
    m_ref[...] = m_new

    hs = hs * lax.rsqrt(jnp.mean(hs * hs, axis=-1, keepdims=True) + NORM_EPS)
    o_ref[...] = (hs * hn_ref[...] * _sigmoid(og_ref[...].astype(F32))).astype(o_ref.dtype)


def mlstm_core(proj, gates, gate_bias, head_norm, batch, seq_len, nh, dk, dv):
    T = proj.shape[0]
    L = min(MLSTM_CHUNK, seq_len)
    nc = seq_len // L
    g2 = gates[:, :2 * nh]
    g_rows = g2.reshape(batch, nc, L, 2 * nh).transpose(0, 3, 1, 2)
    k_off = (nh * dk) // dk
    v_off = (2 * nh * dk) // dv
    og_off = (2 * nh * dk + nh * dv) // dv
    row = lambda bh, c: (bh // nh) * nc + c
    kern = functools.partial(_mlstm_kernel, nh=nh, L=L, dk=dk)
    return pl.pallas_call(
        kern,
        out_shape=jax.ShapeDtypeStruct((T, nh * dv), BF16),
        grid=(batch * nh, nc),
        in_specs=[pl.BlockSpec(memory_space=pltpu.SMEM),
                  pl.BlockSpec((L, dk), lambda bh, c: (row(bh, c), bh % nh)),
                  pl.BlockSpec((L, dk), lambda bh, c: (row(bh, c), k_off + bh % nh)),
                  pl.BlockSpec((L, dv), lambda bh, c: (row(bh, c), v_off + bh % nh)),
                  pl.BlockSpec((L, dv), lambda bh, c: (row(bh, c), og_off + bh % nh)),
                  pl.BlockSpec((L, 2 * nh), lambda bh, c: (row(bh, c), 0)),
                  pl.BlockSpec((None, None, nc, L), lambda bh, c: (bh // nh, bh % nh, 0, 0)),
                  pl.BlockSpec((None, None, nc, L), lambda bh, c: (bh // nh, nh + bh % nh, 0, 0)),
                  pl.BlockSpec((1, dv), lambda bh, c: (0, bh % nh))],
        out_specs=pl.BlockSpec((L, dv), lambda bh, c: (row(bh, c), bh % nh)),
        scratch_shapes=[pltpu.VMEM((dk, dv), F32), pltpu.VMEM((1, dk), F32), pltpu.VMEM((1, 1), F32)],
        compiler_params=_cparams(2),
        name="mlstm_core",
    )(gate_bias, proj, proj, proj, proj, g2, g_rows, g_rows, head_norm)


def _sb_kernel(q_ref, k_ref, v_ref, o_ref, acc_ref, run_ref, *, tq, tk, dh, group):
    qi = pl.program_id(1)
    nsub = tq // tk
    later = (lax.broadcasted_iota(jnp.int32, (tk, tk), 0)
             > lax.broadcasted_iota(jnp.int32, (tk, tk), 1)).astype(BF16)

    acc_ref[...] = jnp.zeros_like(acc_ref)
    run_ref[...] = jnp.zeros_like(run_ref)

    def tile(kb, row0, masked):
        start = pl.multiple_of(kb * tk, tk)
        rows = slice(row0, tq)
        if masked:
            causal = (lax.broadcasted_iota(jnp.int32, (tq - row0, tk), 1)
                      < lax.broadcasted_iota(jnp.int32, (tq - row0, tk), 0))
        for g in range(group):
            cols = slice(g * dh, (g + 1) * dh)
            kt = k_ref[pl.ds(start, tk), cols]
            vt = v_ref[pl.ds(start, tk), cols]
            zn = lax.dot_general(q_ref[rows, cols], kt, (((1,), (1,)), ((), ())),
                                 preferred_element_type=F32)
            neg_abs = pltpu.bitcast(pltpu.bitcast(zn, jnp.uint32) | jnp.uint32(0x80000000), F32)
            log_1mb = jnp.minimum(zn, 0.0) - jnp.log2(1.0 + jnp.exp2(neg_abs))
            if masked:
                log_1mb = jnp.where(causal, log_1mb, 0.0)
            suffix = _dot(log_1mb.astype(BF16), later)
            e = jnp.exp2((log_1mb - zn) + suffix)
            if masked:
                e = jnp.where(causal, e, 0.0)
            run = run_ref[g, rows, :]
            acc_ref[g, rows, :] += jnp.exp2(run) * _dot(e.astype(BF16), vt)
            run_ref[g, rows, :] = run + jnp.sum(log_1mb, axis=1, keepdims=True)

    for d in reversed(range(nsub)):
        tile(qi * nsub + d, d * tk, True)

    def body(it, carry):
        tile(qi * nsub - 1 - it, 0, False)
        return carry

    lax.fori_loop(0, qi * nsub, body, 0)
    for g in range(group):
        o_ref[:, g * dh:(g + 1) * dh] = acc_ref[g].astype(o_ref.dtype)


def stick_breaking(q, kv, batch, seq_len, heads):
    T = q.shape[0]
    dh = SB_HEAD_DIM
    tq = min(ATT_Q_BLOCK, seq_len)
    tk = min(ATT_K_BLOCK, tq)
    nq = seq_len // tq
    group = min(ATT_HEAD_GROUP, heads)
    ng = heads // group
    gw = group * dh
    return pl.pallas_call(
        functools.partial(_sb_kernel, tq=tq, tk=tk, dh=dh, group=group),
        out_shape=jax.ShapeDtypeStruct((T, heads * dh), BF16),
        grid=(batch * ng, nq),
        in_specs=[pl.BlockSpec((tq, gw), lambda g, i: ((g // ng) * nq + i, g % ng)),
                  pl.BlockSpec((seq_len, gw), lambda g, i: (g // ng, g % ng)),
                  pl.BlockSpec((seq_len, gw), lambda g, i: (g // ng, ng + g % ng))],
        out_specs=pl.BlockSpec((tq, gw), lambda g, i: ((g // ng) * nq + i, g % ng)),
        scratch_shapes=[pltpu.VMEM((group, tq, dh), F32), pltpu.VMEM((group, tq, 1), F32)],
        compiler_params=_cparams(2),
        name="stick_breaking",
    )(q, kv, kv)


def _pad_cols(a, n):
    return jnp.pad(a, [(0, 0)] * (a.ndim - 1) + [(0, n - a.shape[-1])])


def _split_pad(a, f, fp):
    return jnp.concatenate([_pad_cols(a[..., :f], fp), _pad_cols(a[..., f:], fp)], axis=-1)


def _prep_up_kernel(g_ref, a_ref, b_ref, og_ref, ov_ref, *, f, wcol, shift):
    col = pl.program_id(2) * wcol + lax.broadcasted_iota(jnp.int32, g_ref.shape, 1)
    valid = col < f
    og_ref[...] = jnp.where(valid, g_ref[...], 0.0).astype(BF16)
    val = jnp.concatenate([a_ref[:, shift:], b_ref[:, :shift]], axis=1)
    ov_ref[...] = jnp.where(valid, val, 0.0).astype(BF16)


def prep_up_weights(w_up, f, fp):
    layers, D, two_f = w_up.shape
    shift = f % LANES
    base = f - shift
    wcol = PREP_COLS
    if shift == 0 or base % wcol != 0:
        wb = w_up.astype(BF16)
        return _pad_cols(wb[..., :f], fp), _pad_cols(wb[..., f:], fp)
    tr = min(PREP_ROWS, D)
    nj = -(-fp // wcol)
    last_tile = (two_f - 1) // LANES
    tiles = wcol // LANES
    out = jax.ShapeDtypeStruct((layers, D, fp), BF16)
    ospec = pl.BlockSpec((None, tr, wcol), lambda l, r, j: (l, r, j))
    return pl.pallas_call(
        functools.partial(_prep_up_kernel, f=f, wcol=wcol, shift=shift),
        out_shape=(out, out),
        grid=(layers, D // tr, nj),
        in_specs=[pl.BlockSpec((None, tr, wcol), lambda l, r, j: (l, r, j)),
                  pl.BlockSpec((None, tr, wcol), lambda l, r, j: (l, r, base // wcol + j)),
                  pl.BlockSpec((None, tr, LANES), lambda l, r, j: (
                      l, r, jnp.minimum((base // wcol + j + 1) * tiles, last_tile)))],
        out_specs=(ospec, ospec),
        compiler_params=_cparams(3),
        name="prep_up_weights",
    )(w_up, w_up, w_up)


def kernel(x, p, a_norm_pre, a_w_in, a_gate_bias, a_head_norm, a_w_out, a_norm_post, kv_norm, kv_w, b_norm_pre, b_w_q, b_w_out, b_norm_post, f_norm_pre, f_w_up, f_conv_w, f_conv_b, f_w_down, f_norm_post, ple_w, ple_gate_norm, ple_gate_w):
    B, S, D = x.shape
    T = B * S
    depth = p.shape[0]
    n_a = a_w_in.shape[0]
    nh = a_gate_bias.shape[1] // 2
    v_w = a_w_out.shape[1]
    dv = v_w // nh
    qk_w = (a_w_in.shape[2] - 2 * v_w - 2 * nh) // 2
    dk = qk_w // nh
    n_main = 2 * qk_w + 2 * v_w
    sb_heads = b_w_q.shape[2] // SB_HEAD_DIM
    F = f_w_down.shape[1]
    Fp = -(-F // TK) * TK
    row = lambda g: g.reshape(1, -1)

    w_in = a_w_in.astype(BF16)
    w_gates = _pad_cols(a_w_in[:, :, n_main:], LANES).astype(BF16)
    w_a_out = a_w_out.astype(BF16)
    w_kv = kv_w.astype(BF16)[None]
    w_q = b_w_q.astype(BF16)
    w_b_out = b_w_out.astype(BF16)
    w_up_gate, w_up_val = prep_up_weights(f_w_up, F, Fp)
    conv_w = _split_pad(f_conv_w, F, Fp)
    conv_b = _split_pad(f_conv_b, F, Fp)
    w_down = jnp.pad(f_w_down.astype(BF16), ((0, 0), (0, Fp - F), (0, 0)))
    w_ple_gate = ple_gate_w.astype(BF16)
    w_ple = ple_w.astype(BF16)
    p2 = p.reshape(depth, T, -1)

    h = x.reshape(T, D)
    kv = None
    for layer in range(depth):
        if layer < n_a:
            a = layer
            proj = norm_matmul(h, row(a_norm_pre[a]), w_in, a, BF16, n_cols=n_main)
            gates = norm_matmul(h, row(a_norm_pre[a]), w_gates, a, F32)
            mixed = mlstm_core(proj, gates, a_gate_bias[a], row(a_head_norm[a]), B, S, nh, dk, dv)
            h = matmul_norm_residual(mixed, w_a_out, a, row(a_norm_post[a]), h)
        else:
            j = layer - n_a
            if kv is None:
                kv = norm_matmul(h, row(kv_norm), w_kv, 0, BF16)
            q = norm_matmul(h, row(b_norm_pre[j]), w_q, j, BF16,
                            scale=-LOG2_E * SB_HEAD_DIM ** -0.5)
            mixed = stick_breaking(q, kv, B, S, sb_heads)
            h = matmul_norm_residual(mixed, w_b_out, j, row(b_norm_post[j]), h)

        act = ffn_up(h, row(f_norm_pre[layer]), w_up_gate, w_up_val, layer,
                     conv_w[layer], row(conv_b[layer]), S)
        h = matmul_norm_residual(act, w_down, layer, row(f_norm_post[layer]), h)
        h = per_layer_embed(h, row(ple_gate_norm[layer]), w_ple_gate, p2, w_ple, layer)
    return h.reshape(B, S, D)
```

```python
import functools
import math

import jax
import jax.numpy as jnp
from jax import lax
from jax.experimental import pallas as pl
from jax.experimental.pallas import tpu as pltpu

NORM_EPS = 1e-6
LOG2_E = math.log2(math.e)
GATE_SOFTCAP = 15.0
SB_HEAD_DIM = 128
MLSTM_CHUNK = 256
CONV_HALO = 8
PREP_ROWS = 1024
PREP_COLS = 640
LANES = 128
VMEM_LIMIT_BYTES = 60 * 1024 * 1024

TM = 512
TN = 1024
TN_FFN = 512
CN = 512
TK = 1024
ATT_Q_BLOCK = 1024
ATT_K_BLOCK = 256
ATT_HEAD_GROUP = 1

BF16 = jnp.bfloat16
F32 = jnp.float32


def _cparams(n_axes):
    return pltpu.CompilerParams(dimension_semantics=("arbitrary",) * n_axes,
                                vmem_limit_bytes=VMEM_LIMIT_BYTES)


def _rms(x, g):
    return x * lax.rsqrt(jnp.mean(x * x, axis=-1, keepdims=True) + NORM_EPS) * g


def _dot(a, b):
    return jnp.dot(a, b, preferred_element_type=F32)


def _sigmoid(x):
    return 1.0 / (1.0 + jnp.exp(-x))


def _log_sigmoid(x):
    return jnp.minimum(x, 0.0) - jnp.log(1.0 + jnp.exp(-jnp.abs(x)))


def _softcap(x):
    return GATE_SOFTCAP * jnp.tanh(x / GATE_SOFTCAP)


def _nmm_kernel(x_ref, g_ref, w_ref, o_ref, xn_ref, *, scale):
    @pl.when(pl.program_id(1) == 0)
    def _():
        xn_ref[...] = _rms(x_ref[...], g_ref[...]).astype(BF16)

    y = _dot(xn_ref[...], w_ref[...])
    if scale != 1.0:
        y = y * scale
    o_ref[...] = y.astype(o_ref.dtype)


def _layer_spec(block, index_map, layer):
    return pl.BlockSpec((None,) + block, lambda *idx: (layer,) + index_map(*idx))


def norm_matmul(x, g, w, layer, out_dtype, scale=1.0, n_cols=None):
    T, D = x.shape
    N = w.shape[2] if n_cols is None else n_cols
    tm, tn = min(TM, T), min(TN, N)
    return pl.pallas_call(
        functools.partial(_nmm_kernel, scale=scale),
        out_shape=jax.ShapeDtypeStruct((T, N), out_dtype),
        grid=(T // tm, N // tn),
        in_specs=[pl.BlockSpec((tm, D), lambda i, j: (i, 0)),
                  pl.BlockSpec((1, D), lambda i, j: (0, 0)),
                  _layer_spec((D, tn), lambda i, j: (0, j), layer)],
        out_specs=pl.BlockSpec((tm, tn), lambda i, j: (i, j)),
        scratch_shapes=[pltpu.VMEM((tm, D), BF16)],
        compiler_params=_cparams(2),
        name="norm_matmul",
    )(x, g, w)


def _ffn_up_kernel(x_ref, g_ref, wg_ref, wv_ref, cwg_ref, cwv_ref, cbg_ref, cbv_ref,
                   o_ref, xn_ref, ug_ref, uv_ref, hg_ref, hv_ref, *, tm, tiles_per_seq):
    i = pl.program_id(0)
    j = pl.program_id(1)
    h0 = CONV_HALO

    @pl.when(j == 0)
    def _():
        xn_ref[...] = _rms(x_ref[...], g_ref[...]).astype(BF16)

    first_of_seq = i % tiles_per_seq == 0

    @pl.when(first_of_seq)
    def _():
        ug_ref[:h0, :] = jnp.zeros((h0, ug_ref.shape[1]), F32)
        uv_ref[:h0, :] = jnp.zeros((h0, uv_ref.shape[1]), F32)

    @pl.when(jnp.logical_not(first_of_seq))
    def _():
        ug_ref[:h0, :] = hg_ref[j]
        uv_ref[:h0, :] = hv_ref[j]

    xn = xn_ref[...]
    ug_ref[h0:, :] = _dot(xn, wg_ref[...])
    uv_ref[h0:, :] = _dot(xn, wv_ref[...])

    def conv(u_ref, cw_ref, cb_ref):
        cw = cw_ref[...]
        return (cw[0:1] * u_ref[h0 - 2:h0 - 2 + tm, :] + cw[1:2] * u_ref[h0 - 1:h0 - 1 + tm, :]
                + cw[2:3] * u_ref[h0:h0 + tm, :] + cb_ref[...])

    gate = conv(ug_ref, cwg_ref, cbg_ref)
    val = conv(uv_ref, cwv_ref, cbv_ref)
    o_ref[...] = (gate * _sigmoid(gate) * val).astype(o_ref.dtype)
    hg_ref[j] = ug_ref[tm:, :]
    hv_ref[j] = uv_ref[tm:, :]


def ffn_up(x, g, wg, wv, layer, cw, cb, seq_len):
    T, D = x.shape
    Fp = wg.shape[2]
    tm, tn = min(TM, T), min(TN_FFN, Fp)
    nj = Fp // tn
    kern = functools.partial(_ffn_up_kernel, tm=tm, tiles_per_seq=seq_len // tm)
    halves = lambda rows: [pl.BlockSpec((rows, tn), lambda i, j: (0, j)),
                           pl.BlockSpec((rows, tn), lambda i, j: (0, nj + j))]
    return pl.pallas_call(
        kern,
        out_shape=jax.ShapeDtypeStruct((T, Fp), BF16),
        grid=(T // tm, nj),
        in_specs=[pl.BlockSpec((tm, D), lambda i, j: (i, 0)),
                  pl.BlockSpec((1, D), lambda i, j: (0, 0)),
                  _layer_spec((D, tn), lambda i, j: (0, j), layer),
                  _layer_spec((D, tn), lambda i, j: (0, j), layer)] + halves(3) + halves(1),
        out_specs=pl.BlockSpec((tm, tn), lambda i, j: (i, j)),
        scratch_shapes=[pltpu.VMEM((tm, D), BF16),
                        pltpu.VMEM((tm + CONV_HALO, tn), F32),
                        pltpu.VMEM((tm + CONV_HALO, tn), F32),
                        pltpu.VMEM((nj, CONV_HALO, tn), F32),
                        pltpu.VMEM((nj, CONV_HALO, tn), F32)],
        compiler_params=_cparams(2),
        name="ffn_up",
    )(x, g, wg, wv, cw, cw, cb, cb)


def _mm_norm_res_kernel(a_ref, w_ref, g_ref, h_ref, o_ref, *, nk, cn):
    k = pl.program_id(1)
    tm, D = o_ref.shape
    chunks = [slice(n * cn, (n + 1) * cn) for n in range(D // cn)]

    @pl.when(k == 0)
    def _():
        o_ref[...] = jnp.zeros_like(o_ref)

    a = a_ref[...]
    for sl in chunks:
        o_ref[:, sl] += _dot(a, w_ref[:, sl])

    @pl.when(k == nk - 1)
    def _():
        ss = jnp.zeros((tm, 1), F32)
        for sl in chunks:
            y = o_ref[:, sl]
            ss = ss + jnp.sum(y * y, axis=-1, keepdims=True)
        inv = lax.rsqrt(ss / D + NORM_EPS)
        for sl in chunks:
            o_ref[:, sl] = h_ref[:, sl] + o_ref[:, sl] * inv * g_ref[:, sl]


def matmul_norm_residual(a, w, layer, g, h):
    T, K = a.shape
    D = w.shape[2]
    tm, tk = min(TM, T), min(TK, K)
    nk = K // tk
    return pl.pallas_call(
        functools.partial(_mm_norm_res_kernel, nk=nk, cn=min(CN, D)),
        out_shape=jax.ShapeDtypeStruct((T, D), F32),
        grid=(T // tm, nk),
        in_specs=[pl.BlockSpec((tm, tk), lambda i, k: (i, k)),
                  _layer_spec((tk, D), lambda i, k: (k, 0), layer),
                  pl.BlockSpec((1, D), lambda i, k: (0, 0)),
                  pl.BlockSpec((tm, D), lambda i, k: (i, 0))],
        out_specs=pl.BlockSpec((tm, D), lambda i, k: (i, 0)),
        compiler_params=_cparams(2),
        name="matmul_norm_residual",
    )(a, w, g, h)


def _ple_kernel(h_ref, g_ref, wg_ref, p_ref, wpe_ref, o_ref, xn_ref, *, tn):
    j = pl.program_id(1)

    @pl.when(j == 0)
    def _():
        xn_ref[...] = _rms(h_ref[...], g_ref[...]).astype(BF16)

    gate = _sigmoid(_dot(xn_ref[...], wg_ref[...]))
    pe = _dot(p_ref[...].astype(BF16), wpe_ref[...])
    col = pl.multiple_of(j * tn, tn)
    o_ref[...] = h_ref[:, pl.ds(col, tn)] + pe * gate


def per_layer_embed(h, g, wg, p, wpe, layer):
    T, D = h.shape
    P = p.shape[2]
    tm, tn = min(TM, T), min(TN, D)
    return pl.pallas_call(
        functools.partial(_ple_kernel, tn=tn),
        out_shape=jax.ShapeDtypeStruct((T, D), F32),
        grid=(T // tm, D // tn),
        in_specs=[pl.BlockSpec((tm, D), lambda i, j: (i, 0)),
                  pl.BlockSpec((1, D), lambda i, j: (0, 0)),
                  _layer_spec((D, tn), lambda i, j: (0, j), layer),
                  _layer_spec((tm, P), lambda i, j: (i, 0), layer),
                  _layer_spec((P, tn), lambda i, j: (0, j), layer)],
        out_specs=pl.BlockSpec((tm, tn), lambda i, j: (i, j)),
        scratch_shapes=[pltpu.VMEM((tm, D), BF16)],
        compiler_params=_cparams(2),
        name="per_layer_embed",
    )(h, g, wg, p, wpe)


def _mlstm_kernel(gb_ref, q_ref, k_ref, v_ref, og_ref, gcol_ref, irow_ref, frow_ref, hn_ref,
                  o_ref, c_ref, n_ref, m_ref, *, nh, L, dk):
    head = pl.program_id(0) % nh
    c = pl.program_id(1)

    @pl.when(c == 0)
    def _():
        c_ref[...] = jnp.zeros_like(c_ref)
        n_ref[...] = jnp.zeros_like(n_ref)
        m_ref[...] = jnp.zeros_like(m_ref)

    bias_i = gb_ref[head]
    bias_f = gb_ref[nh + head]

    i_row = _softcap(irow_ref[pl.ds(c, 1), :] + bias_i)
    f_row = _log_sigmoid(_softcap(frow_ref[pl.ds(c, 1), :] + bias_f))
    gcol = gcol_ref[...]
    lane = lax.broadcasted_iota(jnp.int32, gcol.shape, 1)
    i_col = _softcap(jnp.sum(jnp.where(lane == head, gcol, 0.0), axis=1, keepdims=True) + bias_i)
    f_col = _log_sigmoid(_softcap(
        jnp.sum(jnp.where(lane == nh + head, gcol, 0.0), axis=1, keepdims=True) + bias_f))

    t_idx = lax.broadcasted_iota(jnp.int32, (L, L), 0)
    s_idx = lax.broadcasted_iota(jnp.int32, (L, L), 1)
    causal = s_idx <= t_idx
    b_col = jnp.sum(jnp.where(causal, f_row, 0.0), axis=1, keepdims=True)
    b_row = jnp.sum(jnp.where(t_idx <= s_idx, f_col, 0.0), axis=0, keepdims=True)
    g_tot = jnp.sum(f_row, axis=1, keepdims=True)

    m_prev = m_ref[...]
    dmat = jnp.where(causal, b_col - b_row + i_row, -jnp.inf)
    m_inter = b_col + m_prev
    m_t = jnp.maximum(m_inter, jnp.max(dmat, axis=1, keepdims=True))
    w_intra = jnp.exp(dmat - m_t)
    w_inter = jnp.exp(m_inter - m_t)

    q = q_ref[...]
    k = k_ref[...]
    v = v_ref[...]
    qk_scale = dk ** -0.5
    s_qk = lax.dot_general(q, k, (((1,), (1,)), ((), ())), preferred_element_type=F32)
    p_mat = s_qk * qk_scale * w_intra
    c_state = c_ref[...]
    n_state = n_ref[...]
    num = w_inter * (_dot(q, c_state.astype(BF16)) * qk_scale) + _dot(p_mat.astype(BF16), v)
    qn = jnp.sum(q.astype(F32) * n_state, axis=1, keepdims=True) * qk_scale
    den = w_inter * qn + jnp.sum(p_mat, axis=1, keepdims=True)
    hs = num / jnp.maximum(jnp.abs(den), jnp.exp(-m_t))

    a_col = g_tot - b_col + i_col
    a_row = g_tot - b_row + i_row
    m_new = jnp.maximum(g_tot + m_prev, jnp.max(a_row, axis=1, keepdims=True))
    wa_col = jnp.exp(a_col - m_new)
    decay = jnp.exp(g_tot + m_prev - m_new)
    wv = (wa_col * v.astype(F32)).astype(BF16)
    c_ref[...] = decay * c_state + lax.dot_general(k, wv, (((0,), (0,)), ((), ())),
                                                   preferred_element_type=F32)
    n_ref[...] = decay * n_state + jnp.sum(wa_col * k.astype(F32), axis=0, keepdims=True)
    m_ref[...] = m_new

    hs = hs * lax.rsqrt(jnp.mean(hs * hs, axis=-1, keepdims=True) + NORM_EPS)
    o_ref[...] = (hs * hn_ref[...] * _sigmoid(og_ref[...].astype(F32))).astype(o_ref.dtype)


def mlstm_core(proj, gates, gate_bias, head_norm, batch, seq_len, nh, dk, dv):
    T = proj.shape[0]
    L = min(MLSTM_CHUNK, seq_len)
    nc = seq_len // L
    g2 = gates[:, :2 * nh]
    g_rows = g2.reshape(batch, nc, L, 2 * nh).transpose(0, 3, 1, 2)
    k_off = (nh * dk) // dk
    v_off = (2 * nh * dk) // dv
    og_off = (2 * nh * dk + nh * dv) // dv
    row = lambda bh, c: (bh // nh) * nc + c
    kern = functools.partial(_mlstm_kernel, nh=nh, L=L, dk=dk)
    return pl.pallas_call(
        kern,
        out_shape=jax.ShapeDtypeStruct((T, nh * dv), BF16),
        grid=(batch * nh, nc),
        in_specs=[pl.BlockSpec(memory_space=pltpu.SMEM),
                  pl.BlockSpec((L, dk), lambda bh, c: (row(bh, c), bh % nh)),
                  pl.BlockSpec((L, dk), lambda bh, c: (row(bh, c), k_off + bh % nh)),
                  pl.BlockSpec((L, dv), lambda bh, c: (row(bh, c), v_off + bh % nh)),
                  pl.BlockSpec((L, dv), lambda bh, c: (row(bh, c), og_off + bh % nh)),
                  pl.BlockSpec((L, 2 * nh), lambda bh, c: (row(bh, c), 0)),
                  pl.BlockSpec((None, None, nc, L), lambda bh, c: (bh // nh, bh % nh, 0, 0)),
                  pl.BlockSpec((None, None, nc, L), lambda bh, c: (bh // nh, nh + bh % nh, 0, 0)),
                  pl.BlockSpec((1, dv), lambda bh, c: (0, bh % nh))],
        out_specs=pl.BlockSpec((L, dv), lambda bh, c: (row(bh, c), bh % nh)),
        scratch_shapes=[pltpu.VMEM((dk, dv), F32), pltpu.VMEM((1, dk), F32), pltpu.VMEM((1, 1), F32)],
        compiler_params=_cparams(2),
        name="mlstm_core",
    )(gate_bias, proj, proj, proj, proj, g2, g_rows, g_rows, head_norm)


def _sb_kernel(q_ref, k_ref, v_ref, o_ref, acc_ref, run_ref, *, tq, tk, dh, group):
    qi = pl.program_id(1)
    nsub = tq // tk
    later = (lax.broadcasted_iota(jnp.int32, (tk, tk), 0)
             > lax.broadcasted_iota(jnp.int32, (tk, tk), 1)).astype(BF16)

    acc_ref[...] = jnp.zeros_like(acc_ref)
    run_ref[...] = jnp.zeros_like(run_ref)

    def tile(kb, row0, masked):
        start = pl.multiple_of(kb * tk, tk)
        rows = slice(row0, tq)
        if masked:
            causal = (lax.broadcasted_iota(jnp.int32, (tq - row0, tk), 1)
                      < lax.broadcasted_iota(jnp.int32, (tq - row0, tk), 0))
        for g in range(group):
            cols = slice(g * dh, (g + 1) * dh)
            kt = k_ref[pl.ds(start, tk), cols]
            vt = v_ref[pl.ds(start, tk), cols]
            zn = lax.dot_general(q_ref[rows, cols], kt, (((1,), (1,)), ((), ())),
                                 preferred_element_type=F32)
            neg_abs = pltpu.bitcast(pltpu.bitcast(zn, jnp.uint32) | jnp.uint32(0x80000000), F32)
            log_1mb = jnp.minimum(zn, 0.0) - jnp.log2(1.0 + jnp.exp2(neg_abs))
            if masked:
                log_1mb = jnp.where(causal, log_1mb, 0.0)
            suffix = _dot(log_1mb.astype(BF16), later)
            e = jnp.exp2((log_1mb - zn) + suffix)
            if masked:
                e = jnp.where(causal, e, 0.0)
            run = run_ref[g, rows, :]
            acc_ref[g, rows, :] += jnp.exp2(run) * _dot(e.astype(BF16), vt)
            run_ref[g, rows, :] = run + jnp.sum(log_1mb, axis=1, keepdims=True)

    for d in reversed(range(nsub)):
        tile(qi * nsub + d, d * tk, True)

    def body(it, carry):
        tile(qi * nsub - 1 - it, 0, False)
        return carry

    lax.fori_loop(0, qi * nsub, body, 0)
    for g in range(group):
        o_ref[:, g * dh:(g + 1) * dh] = acc_ref[g].astype(o_ref.dtype)


def stick_breaking(q, kv, batch, seq_len, heads):
    T = q.shape[0]
    dh = SB_HEAD_DIM
    tq = min(ATT_Q_BLOCK, seq_len)
    tk = min(ATT_K_BLOCK, tq)
    nq = seq_len // tq
    group = min(ATT_HEAD_GROUP, heads)
    ng = heads // group
    gw = group * dh
    return pl.pallas_call(
        functools.partial(_sb_kernel, tq=tq, tk=tk, dh=dh, group=group),
        out_shape=jax.ShapeDtypeStruct((T, heads * dh), BF16),
        grid=(batch * ng, nq),
        in_specs=[pl.BlockSpec((tq, gw), lambda g, i: ((g // ng) * nq + i, g % ng)),
                  pl.BlockSpec((seq_len, gw), lambda g, i: (g // ng, g % ng)),
                  pl.BlockSpec((seq_len, gw), lambda g, i: (g // ng, ng + g % ng))],
        out_specs=pl.BlockSpec((tq, gw), lambda g, i: ((g // ng) * nq + i, g % ng)),
        scratch_shapes=[pltpu.VMEM((group, tq, dh), F32), pltpu.VMEM((group, tq, 1), F32)],
        compiler_params=_cparams(2),
        name="stick_breaking",
    )(q, kv, kv)


def _pad_cols(a, n):
    return jnp.pad(a, [(0, 0)] * (a.ndim - 1) + [(0, n - a.shape[-1])])


def _split_pad(a, f, fp):
    return jnp.concatenate([_pad_cols(a[..., :f], fp), _pad_cols(a[..., f:], fp)], axis=-1)


def _prep_up_kernel(g_ref, a_ref, b_ref, og_ref, ov_ref, *, f, wcol, shift):
    col = pl.program_id(2) * wcol + lax.broadcasted_iota(jnp.int32, g_ref.shape, 1)
    valid = col < f
    og_ref[...] = jnp.where(valid, g_ref[...], 0.0).astype(BF16)
    val = jnp.concatenate([a_ref[:, shift:], b_ref[:, :shift]], axis=1)
    ov_ref[...] = jnp.where(valid, val, 0.0).astype(BF16)


def prep_up_weights(w_up, f, fp):
    layers, D, two_f = w_up.shape
    shift = f % LANES
    base = f - shift
    wcol = PREP_COLS
    if shift == 0 or base % wcol != 0:
        wb = w_up.astype(BF16)
        return _pad_cols(wb[..., :f], fp), _pad_cols(wb[..., f:], fp)
    tr = min(PREP_ROWS, D)
    nj = -(-fp // wcol)
    last_tile = (two_f - 1) // LANES
    tiles = wcol // LANES
    out = jax.ShapeDtypeStruct((layers, D, fp), BF16)
    ospec = pl.BlockSpec((None, tr, wcol), lambda l, r, j: (l, r, j))
    return pl.pallas_call(
        functools.partial(_prep_up_kernel, f=f, wcol=wcol, shift=shift),
        out_shape=(out, out),
        grid=(layers, D // tr, nj),
        in_specs=[pl.BlockSpec((None, tr, wcol), lambda l, r, j: (l, r, j)),
                  pl.BlockSpec((None, tr, wcol), lambda l, r, j: (l, r, base // wcol + j)),
                  pl.BlockSpec((None, tr, LANES), lambda l, r, j: (
                      l, r, jnp.minimum((base // wcol + j + 1) * tiles, last_tile)))],
        out_specs=(ospec, ospec),
        compiler_params=_cparams(3),
        name="prep_up_weights",
    )(w_up, w_up, w_up)


def _prep_down_kernel(w_ref, o_ref, *, f, tr):
    row = pl.program_id(1) * tr + lax.broadcasted_iota(jnp.int32, w_ref.shape, 0)
    o_ref[...] = jnp.where(row < f, w_ref[...], 0.0).astype(BF16)


def prep_down_weights(w_down, fp):
    layers, f, D = w_down.shape
    tr = min(TM, fp)
    last = (f - 1) // tr
    return pl.pallas_call(
        functools.partial(_prep_down_kernel, f=f, tr=tr),
        out_shape=jax.ShapeDtypeStruct((layers, fp, D), BF16),
        grid=(layers, fp // tr),
        in_specs=[pl.BlockSpec((None, tr, D), lambda l, r: (l, jnp.minimum(r, last), 0))],
        out_specs=pl.BlockSpec((None, tr, D), lambda l, r: (l, r, 0)),
        compiler_params=_cparams(2),
        name="prep_down_weights",
    )(w_down)


def kernel(x, p, a_norm_pre, a_w_in, a_gate_bias, a_head_norm, a_w_out, a_norm_post, kv_norm, kv_w, b_norm_pre, b_w_q, b_w_out, b_norm_post, f_norm_pre, f_w_up, f_conv_w, f_conv_b, f_w_down, f_norm_post, ple_w, ple_gate_norm, ple_gate_w):
    B, S, D = x.shape
    T = B * S
    depth = p.shape[0]
    n_a = a_w_in.shape[0]
    nh = a_gate_bias.shape[1] // 2
    v_w = a_w_out.shape[1]
    dv = v_w // nh
    qk_w = (a_w_in.shape[2] - 2 * v_w - 2 * nh) // 2
    dk = qk_w // nh
    n_main = 2 * qk_w + 2 * v_w
    sb_heads = b_w_q.shape[2] // SB_HEAD_DIM
    F = f_w_down.shape[1]
    Fp = -(-F // TK) * TK
    row = lambda g: g.reshape(1, -1)

    w_in = a_w_in.astype(BF16)
    w_gates = _pad_cols(a_w_in[:, :, n_main:], LANES).astype(BF16)
    w_a_out = a_w_out.astype(BF16)
    w_kv = kv_w.astype(BF16)[None]
    w_q = b_w_q.astype(BF16)
    w_b_out = b_w_out.astype(BF16)
    w_up_gate, w_up_val = prep_up_weights(f_w_up, F, Fp)
    conv_w = _split_pad(f_conv_w, F, Fp)
    conv_b = _split_pad(f_conv_b, F, Fp)
    w_down = prep_down_weights(f_w_down, Fp)
    w_ple_gate = ple_gate_w.astype(BF16)
    w_ple = ple_w.astype(BF16)
    p2 = p.reshape(depth, T, -1)

    h = x.reshape(T, D)
    kv = None
    for layer in range(depth):
        if layer < n_a:
            a = layer
            proj = norm_matmul(h, row(a_norm_pre[a]), w_in, a, BF16, n_cols=n_main)
            gates = norm_matmul(h, row(a_norm_pre[a]), w_gates, a, F32)
            mixed = mlstm_core(proj, gates, a_gate_bias[a], row(a_head_norm[a]), B, S, nh, dk, dv)
            h = matmul_norm_residual(mixed, w_a_out, a, row(a_norm_post[a]), h)
        else:
            j = layer - n_a
            if kv is None:
                kv = norm_matmul(h, row(kv_norm), w_kv, 0, BF16)
            q = norm_matmul(h, row(b_norm_pre[j]), w_q, j, BF16,
                            scale=-LOG2_E * SB_HEAD_DIM ** -0.5)
            mixed = stick_breaking(q, kv, B, S, sb_heads)
            h = matmul_norm_residual(mixed, w_b_out, j, row(b_norm_post[j]), h)

        act = ffn_up(h, row(f_norm_pre[layer]), w_up_gate, w_up_val, layer,
                     conv_w[layer], row(conv_b[layer]), S)
        h = matmul_norm_residual(act, w_down, layer, row(f_norm_post[layer]), h)
        h = per_layer_embed(h, row(ple_gate_norm[layer]), w_ple_gate, p2, w_ple, layer)
    return h.reshape(B, S, D)
```
